```python
import jax, jax.numpy as jnp
from jax import lax
import numpy as np

D_MODEL = 1024
BATCH = 16
SEQ = 2048
DEPTH = 2

HEAD_DIM = 64
SWA_HEADS = D_MODEL // 128
SWA_KV_HEADS = 2
SWA_GROUP = SWA_HEADS // SWA_KV_HEADS
WINDOW = 128
BLOCK = 128
SB_HEADS = D_MODEL // 128
LRU_WIDTH = D_MODEL // 2
LRU_HEADS = 8
LRU_BLK = LRU_WIDTH // LRU_HEADS
LRU_C = 8.0
CONV_W = 4
N_BRANCHES = 3
D_FF = 7 * D_MODEL // 2
N_EXPERTS = 8
TOP_K = 2
N_DENSE = (DEPTH + 1) // 2
N_MOE = DEPTH // 2
EPS = 1e-6

SWA_Q = SWA_HEADS * HEAD_DIM
SWA_KV = SWA_KV_HEADS * HEAD_DIM
SB_W = SB_HEADS * HEAD_DIM
IN_WIDTHS = (SWA_Q, SWA_KV, SWA_KV, SB_W, SB_W, SB_W, LRU_WIDTH, LRU_WIDTH, N_BRANCHES * D_MODEL)
IN_COLS = sum(IN_WIDTHS)
IN_SPLITS = tuple(int(v) for v in np.cumsum(IN_WIDTHS)[:-1])

kernel_name = "hybrid_swa_stickbreak_rglru_moe"


def rms_norm(x, g):
    xf = x.astype(jnp.float32)
    y = xf * lax.rsqrt(jnp.mean(xf * xf, axis=-1, keepdims=True) + EPS)
    return (y * g.astype(jnp.float32)).astype(x.dtype)


def sliding_window_attention(q, k, v, sinks):
    B, S = q.shape[:2]
    nb = S // BLOCK
    qb = q.reshape(B, nb, BLOCK, SWA_KV_HEADS, SWA_GROUP, HEAD_DIM)

    def band(t):
        tb = t.reshape(B, nb, BLOCK, SWA_KV_HEADS, HEAD_DIM)
        prev = jnp.pad(tb, ((0, 0), (1, 0), (0, 0), (0, 0), (0, 0)))[:, :-1]
        return jnp.concatenate([prev, tb], axis=2)

    kb, vb = band(k), band(v)
    scores = jnp.einsum('bnqhgd,bnkhd->bnhgqk', qb, kb).astype(jnp.float32) * (HEAD_DIM ** -0.5)
    qpos = jnp.arange(BLOCK)[:, None] + BLOCK
    kpos = jnp.arange(2 * BLOCK)[None, :]
    rel = qpos - kpos
    in_window = (rel >= 0) & (rel < WINDOW)
    in_range = (jnp.arange(nb)[:, None, None] > 0) | (kpos[None] >= BLOCK)
    mask = in_window[None] & in_range
    scores = jnp.where(mask[None, :, None, None], scores, -jnp.inf)
    sink = jnp.broadcast_to(
        sinks.astype(jnp.float32).reshape(1, 1, SWA_KV_HEADS, SWA_GROUP, 1, 1),
        scores.shape[:-1] + (1,))
    probs = jax.nn.softmax(jnp.concatenate([scores, sink], axis=-1), axis=-1)[..., :-1]
    out = jnp.einsum('bnhgqk,bnkhd->bnqhgd', probs.astype(v.dtype), vb)
    return out.reshape(B, S, SWA_Q)


def stick_breaking_attention(q, k, v):
    B, S = q.shape[:2]
    scale = HEAD_DIM ** -0.5
    outs = []
    for blk in range(S // BLOCK):
        start, end = blk * BLOCK, (blk + 1) * BLOCK
        z = jnp.einsum('bqhd,bkhd->bhqk', q[:, start:end], k[:, :end]).astype(jnp.float32) * scale
        tpos = start + jnp.arange(BLOCK)[:, None]
        spos = jnp.arange(end)[None, :]
        strict = spos < tpos
        log_fail = jnp.where(strict, jax.nn.log_sigmoid(-z), 0.0)
        later = lax.cumsum(log_fail, axis=log_fail.ndim - 1, reverse=True) - log_fail
        weights = jnp.where(strict, jnp.exp(jax.nn.log_sigmoid(z) + later), 0.0)
        outs.append(jnp.einsum('bhqk,bkhd->bqhd', weights.astype(v.dtype), v[:, :end]))
    return jnp.concatenate(outs, axis=1).reshape(B, S, SB_W)


def causal_depthwise_conv(x, w, b):
    S = x.shape[1]
    xp = jnp.pad(x, ((0, 0), (CONV_W - 1, 0), (0, 0)))
    y = xp[:, 0:S] * w[0]
    for tap in range(1, CONV_W):
        y = y + xp[:, tap:tap + S] * w[tap]
    return y + b


def rg_lru(x, w_r, b_r, w_i, b_i, lam):
    B, S, W = x.shape
    xh = x.reshape(B, S, LRU_HEADS, LRU_BLK)
    r = jax.nn.sigmoid((jnp.einsum('bshi,hij->bshj', xh, w_r).reshape(B, S, W) + b_r).astype(jnp.float32))
    i = jax.nn.sigmoid((jnp.einsum('bshi,hij->bshj', xh, w_i).reshape(B, S, W) + b_i).astype(jnp.float32))
    log_a = -LRU_C * r * jax.nn.softplus(-lam.astype(jnp.float32))
    a = jnp.exp(log_a)
    u = jnp.sqrt(-jnp.expm1(2.0 * log_a)) * (i * x.astype(jnp.float32))

    def combine(left, right):
        a1, b1 = left
        a2, b2 = right
        return a1 * a2, a2 * b1 + b2

    _, h = lax.associative_scan(combine, (a, u), axis=1)
    return h.astype(x.dtype)


def hybrid_mixer(h, w_in, b_gate, q_norm, k_norm, sinks, conv_w, conv_b,
                 lru_w_r, lru_b_r, lru_w_i, lru_b_i, lru_lambda,
                 w_proj_a, w_proj_b, w_proj_c, w_out):
    B, S, _ = h.shape
    proj = h @ w_in
    qa, ka, va, qb, kb, vb, xc, gc, gates = jnp.split(proj, IN_SPLITS, axis=-1)

    qa = rms_norm(qa.reshape(B, S, SWA_HEADS, HEAD_DIM), q_norm)
    ka = rms_norm(ka.reshape(B, S, SWA_KV_HEADS, HEAD_DIM), k_norm)
    va = va.reshape(B, S, SWA_KV_HEADS, HEAD_DIM)
    o_a = sliding_window_attention(qa, ka, va, sinks)

    o_b = stick_breaking_attention(qb.reshape(B, S, SB_HEADS, HEAD_DIM),
                                   kb.reshape(B, S, SB_HEADS, HEAD_DIM),
                                   vb.reshape(B, S, SB_HEADS, HEAD_DIM))

    xc = causal_depthwise_conv(xc, conv_w, conv_b)
    o_c = rg_lru(xc, lru_w_r, lru_b_r, lru_w_i, lru_b_i, lru_lambda) * jax.nn.gelu(gc)

    g = jax.nn.sigmoid(gates + b_gate).reshape(B, S, N_BRANCHES, D_MODEL)
    merged = (g[:, :, 0] * (o_a @ w_proj_a)
              + g[:, :, 1] * (o_b @ w_proj_b)
              + g[:, :, 2] * (o_c @ w_proj_c))
    return merged @ w_out


def swiglu(h, w_gate, w_up, w_down):
    return (jax.nn.silu(h @ w_gate) * (h @ w_up)) @ w_down


def moe_swiglu(h, w_router, w_gate, w_up, w_down):
    B, S, D = h.shape
    hf = h.reshape(B * S, D)
    logits = (hf @ w_router).astype(jnp.float32)
    top_logits, top_idx = lax.top_k(logits, TOP_K)
    top_w = jax.nn.softmax(top_logits, axis=-1)
    combine = jnp.einsum('nk,nke->ne', top_w,
                         jax.nn.one_hot(top_idx, N_EXPERTS, dtype=jnp.float32)).astype(h.dtype)
    out = jnp.zeros_like(hf)
    for e in range(N_EXPERTS):
        out = out + combine[:, e:e + 1] * swiglu(hf, w_gate[e], w_up[e], w_down[e])
    return out.reshape(B, S, D)


def setup_inputs(seed: int = 0) -> dict:
    key = jax.random.key(seed)
    ks = jax.random.split(key, 26)
    f32 = jnp.float32

    def nrm(k, shape, scale):
        return jax.random.normal(k, shape, f32) * scale

    u = jax.random.uniform(ks[13], (DEPTH, LRU_WIDTH), f32, minval=0.9, maxval=0.999)
    p = u ** (1.0 / LRU_C)
    lru_lambda = jnp.log(p) - jnp.log1p(-p)
    return {
        "x": nrm(ks[0], (BATCH, SEQ, D_MODEL), 1.0),
        "attn_norm": 1.0 + nrm(ks[1], (DEPTH, D_MODEL), 0.02),
        "w_in": nrm(ks[2], (DEPTH, D_MODEL, IN_COLS), D_MODEL ** -0.5),
        "b_gate": nrm(ks[3], (DEPTH, N_BRANCHES * D_MODEL), 0.02),
        "q_norm": 1.0 + nrm(ks[4], (DEPTH, HEAD_DIM), 0.02),
        "k_norm": 1.0 + nrm(ks[5], (DEPTH, HEAD_DIM), 0.02),
        "sinks": nrm(ks[6], (DEPTH, SWA_HEADS), 1.0),
        "conv_w": nrm(ks[7], (DEPTH, CONV_W, LRU_WIDTH), CONV_W ** -0.5),
        "conv_b": nrm(ks[8], (DEPTH, LRU_WIDTH), 0.02),
        "lru_w_r": nrm(ks[9], (DEPTH, LRU_HEADS, LRU_BLK, LRU_BLK), LRU_BLK ** -0.5),
        "lru_b_r": nrm(ks[10], (DEPTH, LRU_WIDTH), 0.02),
        "lru_w_i": nrm(ks[11], (DEPTH, LRU_HEADS, LRU_BLK, LRU_BLK), LRU_BLK ** -0.5),
        "lru_b_i": nrm(ks[12], (DEPTH, LRU_WIDTH), 0.02),
        "lru_lambda": lru_lambda,
        "w_proj_a": nrm(ks[14], (DEPTH, SWA_Q, D_MODEL), SWA_Q ** -0.5),
        "w_proj_b": nrm(ks[15], (DEPTH, SB_W, D_MODEL), SB_W ** -0.5),
        "w_proj_c": nrm(ks[16], (DEPTH, LRU_WIDTH, D_MODEL), LRU_WIDTH ** -0.5),
        "w_out": nrm(ks[17], (DEPTH, D_MODEL, D_MODEL), D_MODEL ** -0.5),
        "ffn_norm": 1.0 + nrm(ks[18], (DEPTH, D_MODEL), 0.02),
        "w_ffn_gate": nrm(ks[19], (N_DENSE, D_MODEL, D_FF), D_MODEL ** -0.5),
        "w_ffn_up": nrm(ks[20], (N_DENSE, D_MODEL, D_FF), D_MODEL ** -0.5),
        "w_ffn_down": nrm(ks[21], (N_DENSE, D_FF, D_MODEL), D_FF ** -0.5),
        "w_router": nrm(ks[22], (N_MOE, D_MODEL, N_EXPERTS), D_MODEL ** -0.5),
        "w_exp_gate": nrm(ks[23], (N_MOE, N_EXPERTS, D_MODEL, D_FF), D_MODEL ** -0.5),
        "w_exp_up": nrm(ks[24], (N_MOE, N_EXPERTS, D_MODEL, D_FF), D_MODEL ** -0.5),
        "w_exp_down": nrm(ks[25], (N_MOE, N_EXPERTS, D_FF, D_MODEL), D_FF ** -0.5),
    }


def reference(x, attn_norm, w_in, b_gate, q_norm, k_norm, sinks, conv_w, conv_b,
              lru_w_r, lru_b_r, lru_w_i, lru_b_i, lru_lambda,
              w_proj_a, w_proj_b, w_proj_c, w_out, ffn_norm,
              w_ffn_gate, w_ffn_up, w_ffn_down,
              w_router, w_exp_gate, w_exp_up, w_exp_down):
    for l in range(DEPTH):
        h = rms_norm(x, attn_norm[l])
        x = x + hybrid_mixer(h, w_in[l], b_gate[l], q_norm[l], k_norm[l], sinks[l],
                             conv_w[l], conv_b[l], lru_w_r[l], lru_b_r[l],
                             lru_w_i[l], lru_b_i[l], lru_lambda[l],
                             w_proj_a[l], w_proj_b[l], w_proj_c[l], w_out[l])
        h = rms_norm(x, ffn_norm[l])
        if l % 2 == 0:
            j = l // 2
            x = x + swiglu(h, w_ffn_gate[j], w_ffn_up[j], w_ffn_down[j])
        else:
            j = l // 2
            x = x + moe_swiglu(h, w_router[j], w_exp_gate[j], w_exp_up[j], w_exp_down[j])
    return x
```

```python
import functools

import numpy as np
import jax
import jax.numpy as jnp
from jax import lax
from jax.experimental import pallas as pl
from jax.experimental.pallas import tpu as pltpu

F32 = jnp.float32
BF16 = jnp.bfloat16

HEAD_DIM = 64
SWA_KV_HEADS = 2
WINDOW = 128
LRU_HEADS = 8
LRU_C = 8.0
CONV_W = 4
N_BRANCHES = 3
TOP_K = 2
EPS = 1e-6

LANES = 128
SUBLANES = 8
VMEM_LIMIT = 56 * 1024 * 1024
NEG_BIG = -1e30

SB_QBLK = 256
LRU_CHUNK = 512
ROW_TILE = 512
MOE_TILE = 512


def _cparams(*sem):
    return pltpu.CompilerParams(dimension_semantics=sem, vmem_limit_bytes=VMEM_LIMIT)


def _resident(shape):
    nd = len(shape)
    return pl.BlockSpec(shape, lambda *_: (0,) * nd, pipeline_mode=pl.Buffered(1))


def _dot(a, b):
    return jnp.dot(a, b, preferred_element_type=F32)


def _dot_nt(a, b):
    return lax.dot_general(a, b, (((1,), (1,)), ((), ())), preferred_element_type=F32)


def _split_dot(a, b_bf16):
    hi = a.astype(BF16)
    lo = (a - hi.astype(F32)).astype(BF16)
    return _dot(hi, b_bf16) + _dot(lo, b_bf16)


def _rms(x, gain):
    ms = jnp.mean(x * x, axis=-1, keepdims=True)
    return x * lax.rsqrt(ms + EPS) * gain


def _in_proj_kernel(widths, x_ref, g_ref, w_ref, *out_refs):
    h = _rms(x_ref[...], g_ref[...]).astype(BF16)
    off = 0
    for ref, width in zip(out_refs, widths):
        for c in range(0, width, 1024):
            cw = min(1024, width - c)
            ref[:, c:c + cw] = _dot(h, w_ref[:, off + c:off + c + cw]).astype(ref.dtype)
        off += width


def _in_proj(x2, gain, w_bf16, widths):
    n, d = x2.shape
    tm = min(ROW_TILE, n)
    return pl.pallas_call(
        functools.partial(_in_proj_kernel, widths),
        grid=(n // tm,),
        in_specs=[pl.BlockSpec((tm, d), lambda i: (i, 0)),
                  _resident((1, d)),
                  _resident(w_bf16.shape)],
        out_specs=[pl.BlockSpec((tm, w), lambda i: (i, 0)) for w in widths],
        out_shape=[jax.ShapeDtypeStruct((n, w), BF16) for w in widths],
        compiler_params=_cparams("parallel"),
        name="in_proj",
    )(x2, gain, w_bf16)


def _head_norm(xf, bd, gain):
    ms = _split_dot(xf * xf, bd)
    return xf * lax.rsqrt(ms + EPS) * gain


def _swa_kernel(n_heads, q_ref, kc_ref, kp_ref, vc_ref, vp_ref, bdq_ref, bdk_ref,
                qg_ref, kg_ref, sink_ref, o_ref):
    n = pl.program_id(1)
    blk = q_ref.shape[0]
    group = n_heads // SWA_KV_HEADS
    qn = _head_norm(q_ref[...].astype(F32), bdq_ref[...], qg_ref[...]).astype(BF16)
    k = jnp.concatenate([kp_ref[...], kc_ref[...]], axis=0).astype(F32)
    kn = _head_norm(k, bdk_ref[...], kg_ref[...]).astype(BF16)
    v = jnp.concatenate([vp_ref[...], vc_ref[...]], axis=0)
    k_var = (kn, pltpu.roll(kn, HEAD_DIM, 1))
    v_var = (v, pltpu.roll(v, HEAD_DIM, 1))

    qpos = lax.broadcasted_iota(jnp.int32, (blk, 2 * blk), 0) + blk
    kpos = lax.broadcasted_iota(jnp.int32, (blk, 2 * blk), 1)
    rel = qpos - kpos
    mask = (rel >= 0) & (rel < WINDOW) & ((n > 0) | (kpos >= blk))
    lane = lax.broadcasted_iota(jnp.int32, (blk, LANES), 1)
    low = lane < HEAD_DIM

    for pair in range(n_heads // 2):
        q_pair = qn[:, pair * LANES:(pair + 1) * LANES]
        outs = []
        for parity in range(2):
            j = 2 * pair + parity
            h = j // group
            qm = jnp.where(low if parity == 0 else ~low, q_pair, jnp.zeros_like(q_pair))
            s = _dot_nt(qm, k_var[(h + parity) % 2])
            s = jnp.where(mask, s, NEG_BIG)
            sink = sink_ref[j]
            m = jnp.maximum(jnp.max(s, axis=-1, keepdims=True), sink)
            p = jnp.exp(s - m)
            denom = jnp.sum(p, axis=-1, keepdims=True) + jnp.exp(sink - m)
            o = _dot(p.astype(BF16), v_var[(h + parity) % 2])
            outs.append(o / denom)
        o_ref[:, pair * LANES:(pair + 1) * LANES] = jnp.where(low, outs[0], outs[1]).astype(o_ref.dtype)


def _block_diag_mean(width):
    idx = np.arange(width) // HEAD_DIM
    return jnp.asarray((idx[:, None] == idx[None, :]).astype(np.float32) / HEAD_DIM, BF16)


def _swa(qa, ka, va, q_norm, k_norm, sinks, batch, seq):
    n, qw = qa.shape
    kw = ka.shape[1]
    n_heads = qw // HEAD_DIM
    blk = WINDOW
    nb = seq // blk
    qg = (jnp.tile(q_norm.astype(F32), n_heads) * (HEAD_DIM ** -0.5)).reshape(1, qw)
    kg = jnp.tile(k_norm.astype(F32), SWA_KV_HEADS).reshape(1, kw)
    cur = lambda b, i: (b * nb + i, 0)
    prev = lambda b, i: (b * nb + jnp.maximum(i - 1, 0), 0)
    return pl.pallas_call(
        functools.partial(_swa_kernel, n_heads),
        grid=(batch, nb),
        in_specs=[pl.BlockSpec((blk, qw), cur),
                  pl.BlockSpec((blk, kw), cur), pl.BlockSpec((blk, kw), prev),
                  pl.BlockSpec((blk, kw), cur), pl.BlockSpec((blk, kw), prev),
                  _resident((qw, qw)), _resident((kw, kw)),
                  _resident((1, qw)), _resident((1, kw)),
                  pl.BlockSpec(memory_space=pltpu.SMEM)],
        out_specs=pl.BlockSpec((blk, qw), cur),
        out_shape=jax.ShapeDtypeStruct((n, qw), BF16),
        compiler_params=_cparams("parallel", "parallel"),
        name="swa",
    )(qa, ka, ka, va, va, _block_diag_mean(qw), _block_diag_mean(kw), qg, kg, sinks.astype(F32))


def _sb_kernel(q_ref, k_ref, v_ref, tri_ref, o_ref):
    i = pl.program_id(2)
    qb = q_ref.shape[0]
    q = q_ref[...]
    lane = lax.broadcasted_iota(jnp.int32, (qb, LANES), 1)
    low = lane < HEAD_DIM
    zero = jnp.zeros_like(q)
    scale = jnp.asarray(HEAD_DIM ** -0.5, q.dtype)
    q2 = jnp.concatenate([jnp.where(low, q, zero), jnp.where(low, zero, q)], axis=0) * scale
    tri = tri_ref[...]

    def chunk(c, carry, acc, diagonal):
        start = pl.multiple_of(c * qb, qb)
        k = k_ref[pl.ds(start, qb), :]
        v = v_ref[pl.ds(start, qb), :]
        z = _dot_nt(q2, k)
        sp = jnp.maximum(z, 0.0) + jnp.log(1.0 + jnp.exp(-jnp.abs(z)))
        if diagonal:
            row = jnp.bitwise_and(lax.broadcasted_iota(jnp.int32, (2 * qb, qb), 0), qb - 1)
            col = lax.broadcasted_iota(jnp.int32, (2 * qb, qb), 1)
            strict = col < row
            spm = jnp.where(strict, sp, 0.0)
        else:
            spm = sp
        later = _split_dot(spm, tri)
        w = jnp.exp(z - sp - later - carry)
        if diagonal:
            w = jnp.where(strict, w, 0.0)
        carry = carry + jnp.sum(spm, axis=-1, keepdims=True)
        acc = acc + _dot(w.astype(BF16), v)
        return carry, acc

    carry = jnp.zeros((2 * qb, 1), F32)
    acc = jnp.zeros((2 * qb, LANES), F32)
    carry, acc = chunk(i, carry, acc, True)

    def body(t, state):
        return chunk(i - 1 - t, state[0], state[1], False)

    carry, acc = lax.fori_loop(0, i, body, (carry, acc))
    o_ref[...] = jnp.where(low, acc[:qb], acc[qb:]).astype(o_ref.dtype)


def _stick_breaking(qb, kb, vb, batch, seq):
    n, w = qb.shape
    blk = min(SB_QBLK, seq)
    nq = seq // blk
    tri = jnp.asarray(np.tril(np.ones((blk, blk), np.float32), -1), BF16)
    return pl.pallas_call(
        _sb_kernel,
        grid=(batch, w // LANES, nq),
        in_specs=[pl.BlockSpec((blk, LANES), lambda b, p, i: (b * nq + i, p)),
                  pl.BlockSpec((seq, LANES), lambda b, p, i: (b, p)),
                  pl.BlockSpec((seq, LANES), lambda b, p, i: (b, p)),
                  _resident((blk, blk))],
        out_specs=pl.BlockSpec((blk, LANES), lambda b, p, i: (b * nq + i, p)),
        out_shape=jax.ShapeDtypeStruct((n, w), BF16),
        compiler_params=_cparams("parallel", "parallel", "arbitrary"),
        name="stick_breaking",
    )(qb, kb, vb, tri)


def _gelu_tanh(x):
    return 0.5 * x * (1.0 + jnp.tanh(np.sqrt(2.0 / np.pi) * (x + 0.044715 * (x * x * x))))


def _lru_kernel(x_ref, g_ref, cw_ref, cb_ref, wri_ref, bri_ref, lam_ref, o_ref, tail_ref, h_ref):
    j = pl.program_id(1)
    ts, width = x_ref.shape

    @pl.when(j == 0)
    def _():
        tail_ref[...] = jnp.zeros_like(tail_ref)
        h_ref[...] = jnp.zeros_like(h_ref)

    x = x_ref[...].astype(F32)
    tail = tail_ref[...]
    row8 = lax.broadcasted_iota(jnp.int32, (SUBLANES, width), 0)
    y = x * cw_ref[CONV_W - 1:CONV_W, :] + cb_ref[...]
    for d in range(1, CONV_W):
        xs = pltpu.roll(x, d, 0)
        top = jnp.where(row8 < d, pltpu.roll(tail, d, 0), xs[:SUBLANES])
        xs = jnp.concatenate([top, xs[SUBLANES:]], axis=0)
        y = y + xs * cw_ref[CONV_W - 1 - d:CONV_W - d, :]
    tail_ref[...] = x[ts - SUBLANES:]

    ri = _dot(y.astype(BF16), wri_ref[...]) + bri_ref[...]
    r = jax.nn.sigmoid(ri[:, :width])
    gate_i = jax.nn.sigmoid(ri[:, width:])
    lam = lam_ref[...]
    softplus_neg_lam = jnp.maximum(-lam, 0.0) + jnp.log(1.0 + jnp.exp(-jnp.abs(lam)))
    log_a = (-LRU_C) * r * softplus_neg_lam
    a = jnp.exp(log_a)
    u = jnp.sqrt(1.0 - a * a) * (gate_i * y)

    row = lax.broadcasted_iota(jnp.int32, (ts, width), 0)
    d = 1
    while d < ts:
        keep = row >= d
        a_sh = jnp.where(keep, pltpu.roll(a, d, 0), 1.0)
        u_sh = jnp.where(keep, pltpu.roll(u, d, 0), 0.0)
        u = a * u_sh + u
        a = a * a_sh
        d *= 2
    h = u + a * h_ref[...]
    h_ref[...] = h[ts - 1:ts, :]
    o_ref[...] = (h * _gelu_tanh(g_ref[...].astype(F32))).astype(o_ref.dtype)


def _block_diag(w):
    heads, blk, _ = w.shape
    eye = jnp.eye(heads, dtype=w.dtype)
    return jnp.einsum('hij,hg->higj', w, eye).reshape(heads * blk, heads * blk)


def _rglru(xc, gc, conv_w, conv_b, w_r, b_r, w_i, b_i, lam, batch, seq):
    n, width = xc.shape
    ts = min(LRU_CHUNK, seq)
    nc = seq // ts
    wri = jnp.concatenate([_block_diag(w_r), _block_diag(w_i)], axis=1).astype(BF16)
    bri = jnp.concatenate([b_r, b_i]).astype(F32).reshape(1, 2 * width)
    blk = lambda b, j: (b * nc + j, 0)
    return pl.pallas_call(
        _lru_kernel,
        grid=(batch, nc),
        in_specs=[pl.BlockSpec((ts, width), blk), pl.BlockSpec((ts, width), blk),
                  _resident((CONV_W, width)), _resident((1, width)),
                  _resident((width, 2 * width)), _resident((1, 2 * width)),
                  _resident((1, width))],
        out_specs=pl.BlockSpec((ts, width), blk),
        out_shape=jax.ShapeDtypeStruct((n, width), BF16),
        scratch_shapes=[pltpu.VMEM((SUBLANES, width), F32), pltpu.VMEM((1, width), F32)],
        compiler_params=_cparams("parallel", "arbitrary"),
        name="rglru",
    )(xc, gc, conv_w.astype(F32), conv_b.astype(F32).reshape(1, width), wri, bri,
      lam.astype(F32).reshape(1, width))


def _merge_kernel(x_ref, oa_ref, ob_ref, oc_ref, gt_ref, bg_ref, wa_ref, wb_ref, wc_ref, wo_ref, o_ref):
    d = x_ref.shape[1]
    merged = None
    for idx, (b_ref, w_ref) in enumerate(((oa_ref, wa_ref), (ob_ref, wb_ref), (oc_ref, wc_ref))):
        gate = jax.nn.sigmoid(gt_ref[:, idx * d:(idx + 1) * d].astype(F32) + bg_ref[:, idx * d:(idx + 1) * d])
        term = gate * _dot(b_ref[...], w_ref[...])
        merged = term if merged is None else merged + term
    o_ref[...] = x_ref[...] + _dot(merged.astype(BF16), wo_ref[...])


def _merge(x2, o_a, o_b, o_c, gates, b_gate, wa, wb, wc, wo):
    n, d = x2.shape
    tm = min(ROW_TILE, n)
    row = lambda i: (i, 0)
    return pl.pallas_call(
        _merge_kernel,
        grid=(n // tm,),
        in_specs=[pl.BlockSpec((tm, d), row),
                  pl.BlockSpec((tm, o_a.shape[1]), row), pl.BlockSpec((tm, o_b.shape[1]), row),
                  pl.BlockSpec((tm, o_c.shape[1]), row), pl.BlockSpec((tm, gates.shape[1]), row),
                  _resident((1, gates.shape[1])),
                  _resident(wa.shape), _resident(wb.shape), _resident(wc.shape), _resident(wo.shape)],
        out_specs=pl.BlockSpec((tm, d), row),
        out_shape=jax.ShapeDtypeStruct((n, d), F32),
        compiler_params=_cparams("parallel"),
        name="merge",
    )(x2, o_a, o_b, o_c, gates, b_gate.astype(F32).reshape(1, -1), wa, wb, wc, wo)


def _swiglu_tile(h, wg_ref, wu_ref, wd_ref, ff_chunk):
    d_ff = wg_ref.shape[-1]
    acc = None
    for c in range(0, d_ff, ff_chunk):
        g = _dot(h, wg_ref[:, c:c + ff_chunk])
        u = _dot(h, wu_ref[:, c:c + ff_chunk])
        act = (g * jax.nn.sigmoid(g) * u).astype(BF16)
        part = _dot(act, wd_ref[c:c + ff_chunk, :])
        acc = part if acc is None else acc + part
    return acc


def _ffn_chunk(d_ff):
    for c in (512, 896, 256, 128):
        if d_ff % c == 0:
            return c
    return d_ff


def _dense_ffn_kernel(ff_chunk, x_ref, g_ref, wg_ref, wu_ref, wd_ref, o_ref):
    x = x_ref[...]
    h = _rms(x, g_ref[...]).astype(BF16)
    o_ref[...] = x + _swiglu_tile(h, wg_ref, wu_ref, wd_ref, ff_chunk)


def _dense_ffn(x2, gain, wg, wu, wd):
    n, d = x2.shape
    tm = min(ROW_TILE, n)
    return pl.pallas_call(
        functools.partial(_dense_ffn_kernel, _ffn_chunk(wg.shape[1])),
        grid=(n // tm,),
        in_specs=[pl.BlockSpec((tm, d), lambda i: (i, 0)), _resident((1, d)),
                  _resident(wg.shape), _resident(wu.shape), _resident(wd.shape)],
        out_specs=pl.BlockSpec((tm, d), lambda i: (i, 0)),
        out_shape=jax.ShapeDtypeStruct((n, d), F32),
        compiler_params=_cparams("parallel"),
        name="dense_ffn",
    )(x2, gain, wg, wu, wd)


def _router_kernel(n_experts, x_ref, g_ref, wr_ref, h_ref, idx_ref, wt_ref):
    h = _rms(x_ref[...], g_ref[...])
    h_ref[...] = h.astype(h_ref.dtype)
    w = wr_ref[...]
    h1 = h.astype(BF16)
    h2 = (h - h1.astype(F32)).astype(BF16)
    h3 = (h - h1.astype(F32) - h2.astype(F32)).astype(BF16)
    w1 = w.astype(BF16)
    w2 = (w - w1.astype(F32)).astype(BF16)
    w3 = (w - w1.astype(F32) - w2.astype(F32)).astype(BF16)
    logits = (_dot(h1, w1) + (_dot(h1, w2) + _dot(h2, w1))
              + (_dot(h2, w2) + _dot(h1, w3) + _dot(h3, w1)))
    lane = lax.broadcasted_iota(jnp.int32, logits.shape, 1)
    logits = jnp.where(lane < n_experts, logits, NEG_BIG)
    m1 = jnp.max(logits, axis=-1, keepdims=True)
    i1 = jnp.min(jnp.where(logits == m1, lane, LANES), axis=-1, keepdims=True)
    rest = jnp.where(lane == i1, NEG_BIG, logits)
    m2 = jnp.max(rest, axis=-1, keepdims=True)
    i2 = jnp.min(jnp.where(rest == m2, lane, LANES), axis=-1, keepdims=True)
    e2 = jnp.exp(m2 - m1)
    wt1 = 1.0 / (1.0 + e2)
    wt2 = e2 / (1.0 + e2)
    idx_ref[...] = jnp.where(lane == 0, i1, jnp.where(lane == 1, i2, 0))
    wt_ref[...] = jnp.where(lane == 0, wt1, jnp.where(lane == 1, wt2, 0.0))


def _router(x2, gain, w_router):
    n, d = x2.shape
    n_experts = w_router.shape[1]
    tm = min(ROW_TILE, n)
    wr = jnp.zeros((d, LANES), F32).at[:, :n_experts].set(w_router.astype(F32))
    row = lambda i: (i, 0)
    h, idx, wt = pl.pallas_call(
        functools.partial(_router_kernel, n_experts),
        grid=(n // tm,),
        in_specs=[pl.BlockSpec((tm, d), row), _resident((1, d)), _resident((d, LANES))],
        out_specs=[pl.BlockSpec((tm, d), row), pl.BlockSpec((tm, LANES), row), pl.BlockSpec((tm, LANES), row)],
        out_shape=[jax.ShapeDtypeStruct((n, d), BF16), jax.ShapeDtypeStruct((n, LANES), jnp.int32),
                   jax.ShapeDtypeStruct((n, LANES), F32)],
        compiler_params=_cparams("parallel"),
        name="router",
    )(x2, gain, wr)
    return h, idx[:, :TOP_K], wt[:, :TOP_K]


def _moe_kernel(ff_chunk, te_ref, tv_ref, x_ref, s_ref, wg_ref, wu_ref, wd_ref, o_ref):
    t = pl.program_id(0)

    @pl.when(tv_ref[t] > 0)
    def _():
        y = _swiglu_tile(x_ref[...], wg_ref.at[0], wu_ref.at[0], wd_ref.at[0], ff_chunk)
        o_ref[...] = (y * s_ref[...]).astype(o_ref.dtype)

    @pl.when(tv_ref[t] == 0)
    def _():
        o_ref[...] = jnp.zeros_like(o_ref)


def _moe_experts(xs, scale, tile_expert, tile_valid, wg, wu, wd):
    p, d = xs.shape
    d_ff = wg.shape[2]
    tm = MOE_TILE
    single = pl.Buffered(1)
    grid_spec = pltpu.PrefetchScalarGridSpec(
        num_scalar_prefetch=2,
        grid=(p // tm,),
        in_specs=[pl.BlockSpec((tm, d), lambda t, te, tv: (t, 0)),
                  pl.BlockSpec((tm, 1), lambda t, te, tv: (t, 0)),
                  pl.BlockSpec((1, d, d_ff), lambda t, te, tv: (te[t], 0, 0), pipeline_mode=single),
                  pl.BlockSpec((1, d, d_ff), lambda t, te, tv: (te[t], 0, 0), pipeline_mode=single),
                  pl.BlockSpec((1, d_ff, d), lambda t, te, tv: (te[t], 0, 0), pipeline_mode=single)],
        out_specs=pl.BlockSpec((tm, d), lambda t, te, tv: (t, 0)),
    )
    return pl.pallas_call(
        functools.partial(_moe_kernel, _ffn_chunk(d_ff)),
        grid_spec=grid_spec,
        out_shape=jax.ShapeDtypeStruct((p, d), BF16),
        compiler_params=_cparams("arbitrary"),
        name="moe_experts",
    )(tile_expert, tile_valid, xs, scale, wg, wu, wd)


def _combine_kernel(x_ref, y0_ref, y1_ref, o_ref):
    o_ref[...] = x_ref[...] + (y0_ref[...].astype(F32) + y1_ref[...].astype(F32))


def _combine(x2, y0, y1):
    n, d = x2.shape
    tm = min(ROW_TILE, n)
    row = lambda i: (i, 0)
    return pl.pallas_call(
        _combine_kernel,
        grid=(n // tm,),
        in_specs=[pl.BlockSpec((tm, d), row)] * 3,
        out_specs=pl.BlockSpec((tm, d), row),
        out_shape=jax.ShapeDtypeStruct((n, d), F32),
        compiler_params=_cparams("parallel"),
        name="moe_combine",
    )(x2, y0, y1)


def _moe_ffn(x2, gain, w_router, wg, wu, wd):
    n, d = x2.shape
    n_experts = w_router.shape[1]
    tm = MOE_TILE
    h, top_idx, top_w = _router(x2, gain, w_router)

    flat_e = top_idx.reshape(-1)
    onehot = (flat_e[:, None] == jnp.arange(n_experts, dtype=jnp.int32)[None, :]).astype(jnp.int32)
    rank = jnp.take_along_axis(jnp.cumsum(onehot, axis=0), flat_e[:, None], axis=1)[:, 0] - 1
    sizes = jnp.sum(onehot, axis=0)
    padded = ((sizes + tm - 1) // tm) * tm
    starts = jnp.cumsum(padded) - padded
    pos = starts[flat_e] + rank
    n_tiles = (n * TOP_K) // tm + n_experts
    p = n_tiles * tm
    token = jnp.arange(n * TOP_K, dtype=jnp.int32) // TOP_K
    row_src = jnp.zeros((p,), jnp.int32).at[pos].set(token)
    row_scale = jnp.zeros((p,), F32).at[pos].set(top_w.reshape(-1))
    tile_start = jnp.arange(n_tiles, dtype=jnp.int32) * tm
    ends = starts + padded
    tile_expert = jnp.minimum(jnp.sum((tile_start[:, None] >= ends[None, :]).astype(jnp.int32), axis=1),
                              n_experts - 1).astype(jnp.int32)
    tile_valid = (tile_start < ends[-1]).astype(jnp.int32)

    xs = jnp.take(h, row_src, axis=0)
    ys = _moe_experts(xs, row_scale.reshape(p, 1), tile_expert, tile_valid, wg, wu, wd)
    pos2 = pos.reshape(n, TOP_K)
    y0 = jnp.take(ys, pos2[:, 0], axis=0)
    y1 = jnp.take(ys, pos2[:, 1], axis=0)
    return _combine(x2, y0, y1)


def kernel(x, attn_norm, w_in, b_gate, q_norm, k_norm, sinks, conv_w, conv_b, lru_w_r, lru_b_r, lru_w_i, lru_b_i, lru_lambda, w_proj_a, w_proj_b, w_proj_c, w_out, ffn_norm, w_ffn_gate, w_ffn_up, w_ffn_down, w_router, w_exp_gate, w_exp_up, w_exp_down):
    batch, seq, d = x.shape
    depth = w_in.shape[0]
    swa_q = w_proj_a.shape[1]
    sb_w = w_proj_b.shape[1]
    lru_w = w_proj_c.shape[1]
    swa_kv = SWA_KV_HEADS * HEAD_DIM
    widths = (swa_q, swa_kv, swa_kv, sb_w, sb_w, sb_w, lru_w, lru_w, N_BRANCHES * d)
    assert sum(widths) == w_in.shape[2]
    assert seq % SB_QBLK == 0 and seq % WINDOW == 0

    x2 = x.reshape(batch * seq, d).astype(F32)
    for l in range(depth):
        gain = attn_norm[l].astype(F32).reshape(1, d)
        qa, ka, va, qb, kb, vb, xc, gc, gates = _in_proj(x2, gain, w_in[l].astype(BF16), widths)
        o_a = _swa(qa, ka, va, q_norm[l], k_norm[l], sinks[l], batch, seq)
        o_b = _stick_breaking(qb, kb, vb, batch, seq)
        o_c = _rglru(xc, gc, conv_w[l], conv_b[l], lru_w_r[l], lru_b_r[l], lru_w_i[l], lru_b_i[l],
                     lru_lambda[l], batch, seq)
        x2 = _merge(x2, o_a, o_b, o_c, gates, b_gate[l],
                    w_proj_a[l].astype(BF16), w_proj_b[l].astype(BF16), w_proj_c[l].astype(BF16),
                    w_out[l].astype(BF16))
        fgain = ffn_norm[l].astype(F32).reshape(1, d)
        j = l // 2
        if l % 2 == 0:
            x2 = _dense_ffn(x2, fgain, w_ffn_gate[j].astype(BF16), w_ffn_up[j].astype(BF16),
                            w_ffn_down[j].astype(BF16))
        else:
            x2 = _moe_ffn(x2, fgain, w_router[j], w_exp_gate[j].astype(BF16), w_exp_up[j].astype(BF16),
                          w_exp_down[j].astype(BF16))
    return x2.reshape(batch, seq, d).astype(x.dtype)
```

```python
import functools

import numpy as np
import jax
import jax.numpy as jnp
from jax import lax
from jax.experimental import pallas as pl
from jax.experimental.pallas import tpu as pltpu

F32 = jnp.float32
BF16 = jnp.bfloat16

HEAD_DIM = 64
SWA_KV_HEADS = 2
WINDOW = 128
LRU_HEADS = 8
LRU_C = 8.0
CONV_W = 4
N_BRANCHES = 3
TOP_K = 2
EPS = 1e-6

LANES = 128
SUBLANES = 8
VMEM_LIMIT = 56 * 1024 * 1024
NEG_BIG = -1e30

SB_QBLK = 256
SB_UNDERFLOW = 105.0
LRU_CHUNK = 512
ROW_TILE = 512
MOE_TILE = 512


def _cparams(*sem):
    return pltpu.CompilerParams(dimension_semantics=sem, vmem_limit_bytes=VMEM_LIMIT)


def _resident(shape):
    nd = len(shape)
    return pl.BlockSpec(shape, lambda *_: (0,) * nd, pipeline_mode=pl.Buffered(1))


def _dot(a, b):
    return jnp.dot(a, b, preferred_element_type=F32)


def _dot_nt(a, b):
    return lax.dot_general(a, b, (((1,), (1,)), ((), ())), preferred_element_type=F32)


def _split_dot(a, b_bf16):
    hi = a.astype(BF16)
    lo = (a - hi.astype(F32)).astype(BF16)
    return _dot(hi, b_bf16) + _dot(lo, b_bf16)


def _rms(x, gain):
    ms = jnp.mean(x * x, axis=-1, keepdims=True)
    return x * lax.rsqrt(ms + EPS) * gain


def _in_proj_kernel(widths, x_ref, g_ref, w_ref, *out_refs):
    h = _rms(x_ref[...], g_ref[...]).astype(BF16)
    off = 0
    for ref, width in zip(out_refs, widths):
        for c in range(0, width, 1024):
            cw = min(1024, width - c)
            ref[:, c:c + cw] = _dot(h, w_ref[:, off + c:off + c + cw]).astype(ref.dtype)
        off += width


def _in_proj(x2, gain, w_bf16, widths):
    n, d = x2.shape
    tm = min(ROW_TILE, n)
    return pl.pallas_call(
        functools.partial(_in_proj_kernel, widths),
        grid=(n // tm,),
        in_specs=[pl.BlockSpec((tm, d), lambda i: (i, 0)),
                  _resident((1, d)),
                  _resident(w_bf16.shape)],
        out_specs=[pl.BlockSpec((tm, w), lambda i: (i, 0)) for w in widths],
        out_shape=[jax.ShapeDtypeStruct((n, w), BF16) for w in widths],
        compiler_params=_cparams("parallel"),
        name="in_proj",
    )(x2, gain, w_bf16)


def _head_norm(xf, bd, gain):
    ms = _dot((xf * xf).astype(BF16), bd)
    return xf * lax.rsqrt(ms + EPS) * gain


def _swa_kernel(n_heads, q_ref, kc_ref, kp_ref, vc_ref, vp_ref, bdq_ref, bdk_ref,
                qg_ref, kg_ref, sink_ref, o_ref):
    n = pl.program_id(1)
    blk = q_ref.shape[0]
    group = n_heads // SWA_KV_HEADS
    qn = _head_norm(q_ref[...].astype(F32), bdq_ref[...], qg_ref[...]).astype(BF16)
    k = jnp.concatenate([kp_ref[...], kc_ref[...]], axis=0).astype(F32)
    kn = _head_norm(k, bdk_ref[...], kg_ref[...]).astype(BF16)
    v = jnp.concatenate([vp_ref[...], vc_ref[...]], axis=0)
    k_var = (kn, pltpu.roll(kn, HEAD_DIM, 1))
    v_var = (v, pltpu.roll(v, HEAD_DIM, 1))

    qpos = lax.broadcasted_iota(jnp.int32, (blk, 2 * blk), 0) + blk
    kpos = lax.broadcasted_iota(jnp.int32, (blk, 2 * blk), 1)
    rel = qpos - kpos
    mask = (rel >= 0) & (rel < WINDOW) & ((n > 0) | (kpos >= blk))
    lane = lax.broadcasted_iota(jnp.int32, (blk, LANES), 1)
    low = lane < HEAD_DIM

    for pair in range(n_heads // 2):
        q_pair = qn[:, pair * LANES:(pair + 1) * LANES]
        outs = []
        for parity in range(2):
            j = 2 * pair + parity
            h = j // group
            qm = jnp.where(low if parity == 0 else ~low, q_pair, jnp.zeros_like(q_pair))
            s = _dot_nt(qm, k_var[(h + parity) % 2])
            s = jnp.where(mask, s, NEG_BIG)
            sink = sink_ref[j]
            m = jnp.maximum(jnp.max(s, axis=-1, keepdims=True), sink)
            p = jnp.exp(s - m)
            denom = jnp.sum(p, axis=-1, keepdims=True) + jnp.exp(sink - m)
            o = _dot(p.astype(BF16), v_var[(h + parity) % 2])
            outs.append(o / denom)
        o_ref[:, pair * LANES:(pair + 1) * LANES] = jnp.where(low, outs[0], outs[1]).astype(o_ref.dtype)


def _block_diag_mean(width):
    idx = np.arange(width) // HEAD_DIM
    return jnp.asarray((idx[:, None] == idx[None, :]).astype(np.float32) / HEAD_DIM, BF16)


def _swa(qa, ka, va, q_norm, k_norm, sinks, batch, seq):
    n, qw = qa.shape
    kw = ka.shape[1]
    n_heads = qw // HEAD_DIM
    blk = WINDOW
    nb = seq // blk
    qg = (jnp.tile(q_norm.astype(F32), n_heads) * (HEAD_DIM ** -0.5)).reshape(1, qw)
    kg = jnp.tile(k_norm.astype(F32), SWA_KV_HEADS).reshape(1, kw)
    cur = lambda b, i: (b * nb + i, 0)
    prev = lambda b, i: (b * nb + jnp.maximum(i - 1, 0), 0)
    return pl.pallas_call(
        functools.partial(_swa_kernel, n_heads),
        grid=(batch, nb),
        in_specs=[pl.BlockSpec((blk, qw), cur),
                  pl.BlockSpec((blk, kw), cur), pl.BlockSpec((blk, kw), prev),
                  pl.BlockSpec((blk, kw), cur), pl.BlockSpec((blk, kw), prev),
                  _resident((qw, qw)), _resident((kw, kw)),
                  _resident((1, qw)), _resident((1, kw)),
                  pl.BlockSpec(memory_space=pltpu.SMEM)],
        out_specs=pl.BlockSpec((blk, qw), cur),
        out_shape=jax.ShapeDtypeStruct((n, qw), BF16),
        compiler_params=_cparams("parallel", "parallel"),
        name="swa",
    )(qa, ka, ka, va, va, _block_diag_mean(qw), _block_diag_mean(kw), qg, kg, sinks.astype(F32))


def _sb_kernel(q_ref, k_ref, v_ref, tri_ref, o_ref):
    i = pl.program_id(2)
    qb = q_ref.shape[0]
    q = q_ref[...]
    lane = lax.broadcasted_iota(jnp.int32, (qb, LANES), 1)
    low = lane < HEAD_DIM
    zero = jnp.zeros_like(q)
    scale = jnp.asarray(HEAD_DIM ** -0.5, q.dtype)
    q2 = jnp.concatenate([jnp.where(low, q, zero), jnp.where(low, zero, q)], axis=0) * scale
    tri = tri_ref[...]

    def chunk(c, carry, acc, diagonal):
        start = pl.multiple_of(c * qb, qb)
        k = k_ref[pl.ds(start, qb), :]
        v = v_ref[pl.ds(start, qb), :]
        z = _dot_nt(q2, k)
        sp = jnp.maximum(z, 0.0) + jnp.log(1.0 + jnp.exp(-jnp.abs(z)))
        if diagonal:
            row = jnp.bitwise_and(lax.broadcasted_iota(jnp.int32, (2 * qb, qb), 0), qb - 1)
            col = lax.broadcasted_iota(jnp.int32, (2 * qb, qb), 1)
            strict = col < row
            spm = jnp.where(strict, sp, 0.0)
        else:
            spm = sp
        later = _split_dot(spm, tri)
        w = jnp.exp(z - sp - later - carry)
        if diagonal:
            w = jnp.where(strict, w, 0.0)
        carry = carry + jnp.sum(spm, axis=-1, keepdims=True)
        acc = acc + _dot(w.astype(BF16), v)
        return carry, acc

    carry = jnp.zeros((2 * qb, 1), F32)
    acc = jnp.zeros((2 * qb, LANES), F32)
    carry, acc = chunk(i, carry, acc, True)

    def cond(state):
        return (state[0] < i) & (jnp.min(state[1]) <= SB_UNDERFLOW)

    def body(state):
        t, carry, acc = state
        carry, acc = chunk(i - 1 - t, carry, acc, False)
        return t + 1, carry, acc

    _, carry, acc = lax.while_loop(cond, body, (jnp.int32(0), carry, acc))
    o_ref[...] = jnp.where(low, acc[:qb], acc[qb:]).astype(o_ref.dtype)


def _stick_breaking(qb, kb, vb, batch, seq):
    n, w = qb.shape
    blk = min(SB_QBLK, seq)
    nq = seq // blk
    tri = jnp.asarray(np.tril(np.ones((blk, blk), np.float32), -1), BF16)
    return pl.pallas_call(
        _sb_kernel,
        grid=(batch, w // LANES, nq),
        in_specs=[pl.BlockSpec((blk, LANES), lambda b, p, i: (b * nq + i, p)),
                  pl.BlockSpec((seq, LANES), lambda b, p, i: (b, p)),
                  pl.BlockSpec((seq, LANES), lambda b, p, i: (b, p)),
                  _resident((blk, blk))],
        out_specs=pl.BlockSpec((blk, LANES), lambda b, p, i: (b * nq + i, p)),
        out_shape=jax.ShapeDtypeStruct((n, w), BF16),
        compiler_params=_cparams("parallel", "parallel", "arbitrary"),
        name="stick_breaking",
    )(qb, kb, vb, tri)


def _gelu_tanh(x):
    return 0.5 * x * (1.0 + jnp.tanh(np.sqrt(2.0 / np.pi) * (x + 0.044715 * (x * x * x))))


def _lru_kernel(x_ref, g_ref, cw_ref, cb_ref, wri_ref, bri_ref, lam_ref, o_ref, tail_ref, h_ref):
    j = pl.program_id(1)
    ts, width = x_ref.shape

    @pl.when(j == 0)
    def _():
        tail_ref[...] = jnp.zeros_like(tail_ref)
        h_ref[...] = jnp.zeros_like(h_ref)

    x = x_ref[...].astype(F32)
    tail = tail_ref[...]
    row8 = lax.broadcasted_iota(jnp.int32, (SUBLANES, width), 0)
    y = x * cw_ref[CONV_W - 1:CONV_W, :] + cb_ref[...]
    for d in range(1, CONV_W):
        xs = pltpu.roll(x, d, 0)
        top = jnp.where(row8 < d, pltpu.roll(tail, d, 0), xs[:SUBLANES])
        xs = jnp.concatenate([top, xs[SUBLANES:]], axis=0)
        y = y + xs * cw_ref[CONV_W - 1 - d:CONV_W - d, :]
    tail_ref[...] = x[ts - SUBLANES:]

    ri = _dot(y.astype(BF16), wri_ref[...]) + bri_ref[...]
    r = jax.nn.sigmoid(ri[:, :width])
    gate_i = jax.nn.sigmoid(ri[:, width:])
    lam = lam_ref[...]
    softplus_neg_lam = jnp.maximum(-lam, 0.0) + jnp.log(1.0 + jnp.exp(-jnp.abs(lam)))
    log_a = (-LRU_C) * r * softplus_neg_lam
    a = jnp.exp(log_a)
    u = jnp.sqrt(1.0 - a * a) * (gate_i * y)

    row = lax.broadcasted_iota(jnp.int32, (ts, width), 0)
    d = 1
    while d < ts:
        keep = row >= d
        a_sh = jnp.where(keep, pltpu.roll(a, d, 0), 1.0)
        u_sh = jnp.where(keep, pltpu.roll(u, d, 0), 0.0)
        u = a * u_sh + u
        a = a * a_sh
        d *= 2
    h = u + a * h_ref[...]
    h_ref[...] = h[ts - 1:ts, :]
    o_ref[...] = (h * _gelu_tanh(g_ref[...].astype(F32))).astype(o_ref.dtype)


def _block_diag(w):
    heads, blk, _ = w.shape
    eye = jnp.eye(heads, dtype=w.dtype)
    return jnp.einsum('hij,hg->higj', w, eye).reshape(heads * blk, heads * blk)


def _rglru(xc, gc, conv_w, conv_b, w_r, b_r, w_i, b_i, lam, batch, seq):
    n, width = xc.shape
    ts = min(LRU_CHUNK, seq)
    nc = seq // ts
    wri = jnp.concatenate([_block_diag(w_r), _block_diag(w_i)], axis=1).astype(BF16)
    bri = jnp.concatenate([b_r, b_i]).astype(F32).reshape(1, 2 * width)
    blk = lambda b, j: (b * nc + j, 0)
    return pl.pallas_call(
        _lru_kernel,
        grid=(batch, nc),
        in_specs=[pl.BlockSpec((ts, width), blk), pl.BlockSpec((ts, width), blk),
                  _resident((CONV_W, width)), _resident((1, width)),
                  _resident((width, 2 * width)), _resident((1, 2 * width)),
                  _resident((1, width))],
        out_specs=pl.BlockSpec((ts, width), blk),
        out_shape=jax.ShapeDtypeStruct((n, width), BF16),
        scratch_shapes=[pltpu.VMEM((SUBLANES, width), F32), pltpu.VMEM((1, width), F32)],
        compiler_params=_cparams("parallel", "arbitrary"),
        name="rglru",
    )(xc, gc, conv_w.astype(F32), conv_b.astype(F32).reshape(1, width), wri, bri,
      lam.astype(F32).reshape(1, width))


def _merge_kernel(x_ref, oa_ref, ob_ref, oc_ref, gt_ref, bg_ref, wa_ref, wb_ref, wc_ref, wo_ref, o_ref):
    d = x_ref.shape[1]
    merged = None
    for idx, (b_ref, w_ref) in enumerate(((oa_ref, wa_ref), (ob_ref, wb_ref), (oc_ref, wc_ref))):
        gate = jax.nn.sigmoid(gt_ref[:, idx * d:(idx + 1) * d].astype(F32) + bg_ref[:, idx * d:(idx + 1) * d])
        term = gate * _dot(b_ref[...], w_ref[...])
        merged = term if merged is None else merged + term
    o_ref[...] = x_ref[...] + _dot(merged.astype(BF16), wo_ref[...])


def _merge(x2, o_a, o_b, o_c, gates, b_gate, wa, wb, wc, wo):
    n, d = x2.shape
    tm = min(ROW_TILE, n)
    row = lambda i: (i, 0)
    return pl.pallas_call(
        _merge_kernel,
        grid=(n // tm,),
        in_specs=[pl.BlockSpec((tm, d), row),
                  pl.BlockSpec((tm, o_a.shape[1]), row), pl.BlockSpec((tm, o_b.shape[1]), row),
                  pl.BlockSpec((tm, o_c.shape[1]), row), pl.BlockSpec((tm, gates.shape[1]), row),
                  _resident((1, gates.shape[1])),
                  _resident(wa.shape), _resident(wb.shape), _resident(wc.shape), _resident(wo.shape)],
        out_specs=pl.BlockSpec((tm, d), row),
        out_shape=jax.ShapeDtypeStruct((n, d), F32),
        compiler_params=_cparams("parallel"),
        name="merge",
    )(x2, o_a, o_b, o_c, gates, b_gate.astype(F32).reshape(1, -1), wa, wb, wc, wo)


def _swiglu_tile(h, wg_ref, wu_ref, wd_ref, ff_chunk):
    d_ff = wg_ref.shape[-1]
    acc = None
    for c in range(0, d_ff, ff_chunk):
        g = _dot(h, wg_ref[:, c:c + ff_chunk])
        u = _dot(h, wu_ref[:, c:c + ff_chunk])
        act = (g * jax.nn.sigmoid(g) * u).astype(BF16)
        part = _dot(act, wd_ref[c:c + ff_chunk, :])
        acc = part if acc is None else acc + part
    return acc


def _ffn_chunk(d_ff):
    for c in (512, 896, 256, 128):
        if d_ff % c == 0:
            return c
    return d_ff


def _dense_ffn_kernel(ff_chunk, x_ref, g_ref, wg_ref, wu_ref, wd_ref, o_ref):
    x = x_ref[...]
    h = _rms(x, g_ref[...]).astype(BF16)
    o_ref[...] = x + _swiglu_tile(h, wg_ref, wu_ref, wd_ref, ff_chunk)


def _dense_ffn(x2, gain, wg, wu, wd):
    n, d = x2.shape
    tm = min(ROW_TILE, n)
    return pl.pallas_call(
        functools.partial(_dense_ffn_kernel, _ffn_chunk(wg.shape[1])),
        grid=(n // tm,),
        in_specs=[pl.BlockSpec((tm, d), lambda i: (i, 0)), _resident((1, d)),
                  _resident(wg.shape), _resident(wu.shape), _resident(wd.shape)],
        out_specs=pl.BlockSpec((tm, d), lambda i: (i, 0)),
        out_shape=jax.ShapeDtypeStruct((n, d), F32),
        compiler_params=_cparams("parallel"),
        name="dense_ffn",
    )(x2, gain, wg, wu, wd)


def _router_kernel(n_experts, x_ref, g_ref, wr_ref, h_ref, idx_ref, wt_ref):
    h = _rms(x_ref[...], g_ref[...])
    h_ref[...] = h.astype(h_ref.dtype)
    w = wr_ref[...]
    h1 = h.astype(BF16)
    h2 = (h - h1.astype(F32)).astype(BF16)
    h3 = (h - h1.astype(F32) - h2.astype(F32)).astype(BF16)
    w1 = w.astype(BF16)
    w2 = (w - w1.astype(F32)).astype(BF16)
    w3 = (w - w1.astype(F32) - w2.astype(F32)).astype(BF16)
    logits = (_dot(h1, w1) + (_dot(h1, w2) + _dot(h2, w1))
              + (_dot(h2, w2) + _dot(h1, w3) + _dot(h3, w1)))
    lane = lax.broadcasted_iota(jnp.int32, logits.shape, 1)
    logits = jnp.where(lane < n_experts, logits, NEG_BIG)
    m1 = jnp.max(logits, axis=-1, keepdims=True)
    i1 = jnp.min(jnp.where(logits == m1, lane, LANES), axis=-1, keepdims=True)
    rest = jnp.where(lane == i1, NEG_BIG, logits)
    m2 = jnp.max(rest, axis=-1, keepdims=True)
    i2 = jnp.min(jnp.where(rest == m2, lane, LANES), axis=-1, keepdims=True)
    e2 = jnp.exp(m2 - m1)
    wt1 = 1.0 / (1.0 + e2)
    wt2 = e2 / (1.0 + e2)
    idx_ref[...] = jnp.where(lane == 0, i1, jnp.where(lane == 1, i2, 0))
    wt_ref[...] = jnp.where(lane == 0, wt1, jnp.where(lane == 1, wt2, 0.0))


def _router(x2, gain, w_router):
    n, d = x2.shape
    n_experts = w_router.shape[1]
    tm = min(ROW_TILE, n)
    wr = jnp.zeros((d, LANES), F32).at[:, :n_experts].set(w_router.astype(F32))
    row = lambda i: (i, 0)
    h, idx, wt = pl.pallas_call(
        functools.partial(_router_kernel, n_experts),
        grid=(n // tm,),
        in_specs=[pl.BlockSpec((tm, d), row), _resident((1, d)), _resident((d, LANES))],
        out_specs=[pl.BlockSpec((tm, d), row), pl.BlockSpec((tm, LANES), row), pl.BlockSpec((tm, LANES), row)],
        out_shape=[jax.ShapeDtypeStruct((n, d), BF16), jax.ShapeDtypeStruct((n, LANES), jnp.int32),
                   jax.ShapeDtypeStruct((n, LANES), F32)],
        compiler_params=_cparams("parallel"),
        name="router",
    )(x2, gain, wr)
    return h, idx[:, :TOP_K], wt


def _moe_kernel(ff_chunk, te_ref, tv_ref, x_ref, wg_ref, wu_ref, wd_ref, o_ref):
    t = pl.program_id(0)

    @pl.when(tv_ref[t] > 0)
    def _():
        y = _swiglu_tile(x_ref[...], wg_ref.at[0], wu_ref.at[0], wd_ref.at[0], ff_chunk)
        o_ref[...] = y.astype(o_ref.dtype)

    @pl.when(tv_ref[t] == 0)
    def _():
        o_ref[...] = jnp.zeros_like(o_ref)


def _moe_experts(xs, tile_expert, tile_valid, wg, wu, wd):
    p, d = xs.shape
    d_ff = wg.shape[2]
    tm = MOE_TILE
    single = pl.Buffered(1)
    grid_spec = pltpu.PrefetchScalarGridSpec(
        num_scalar_prefetch=2,
        grid=(p // tm,),
        in_specs=[pl.BlockSpec((tm, d), lambda t, te, tv: (t, 0)),
                  pl.BlockSpec((1, d, d_ff), lambda t, te, tv: (te[t], 0, 0), pipeline_mode=single),
                  pl.BlockSpec((1, d, d_ff), lambda t, te, tv: (te[t], 0, 0), pipeline_mode=single),
                  pl.BlockSpec((1, d_ff, d), lambda t, te, tv: (te[t], 0, 0), pipeline_mode=single)],
        out_specs=pl.BlockSpec((tm, d), lambda t, te, tv: (t, 0)),
    )
    return pl.pallas_call(
        functools.partial(_moe_kernel, _ffn_chunk(d_ff)),
        grid_spec=grid_spec,
        out_shape=jax.ShapeDtypeStruct((p, d), BF16),
        compiler_params=_cparams("arbitrary"),
        name="moe_experts",
    )(tile_expert, tile_valid, xs, wg, wu, wd)


def _combine_kernel(x_ref, wt_ref, y0_ref, y1_ref, o_ref):
    wt = wt_ref[...]
    o_ref[...] = x_ref[...] + (wt[:, 0:1] * y0_ref[...].astype(F32) + wt[:, 1:2] * y1_ref[...].astype(F32))


def _combine(x2, wt, y0, y1):
    n, d = x2.shape
    tm = min(ROW_TILE, n)
    row = lambda i: (i, 0)
    return pl.pallas_call(
        _combine_kernel,
        grid=(n // tm,),
        in_specs=[pl.BlockSpec((tm, d), row), pl.BlockSpec((tm, LANES), row),
                  pl.BlockSpec((tm, d), row), pl.BlockSpec((tm, d), row)],
        out_specs=pl.BlockSpec((tm, d), row),
        out_shape=jax.ShapeDtypeStruct((n, d), F32),
        compiler_params=_cparams("parallel"),
        name="moe_combine",
    )(x2, wt, y0, y1)


def _moe_ffn(x2, gain, w_router, wg, wu, wd):
    n, d = x2.shape
    n_experts = w_router.shape[1]
    tm = MOE_TILE
    h, top_idx, top_w = _router(x2, gain, w_router)

    flat_e = top_idx.reshape(-1)
    onehot = (flat_e[:, None] == jnp.arange(n_experts, dtype=jnp.int32)[None, :]).astype(jnp.int32)
    rank = jnp.take_along_axis(jnp.cumsum(onehot, axis=0), flat_e[:, None], axis=1)[:, 0] - 1
    sizes = jnp.sum(onehot, axis=0)
    padded = ((sizes + tm - 1) // tm) * tm
    starts = jnp.cumsum(padded) - padded
    pos = starts[flat_e] + rank
    n_tiles = (n * TOP_K) // tm + n_experts
    p = n_tiles * tm
    token = jnp.arange(n * TOP_K, dtype=jnp.int32) // TOP_K
    row_src = jnp.zeros((p,), jnp.int32).at[pos].set(token, unique_indices=True, mode="promise_in_bounds")
    tile_start = jnp.arange(n_tiles, dtype=jnp.int32) * tm
    ends = starts + padded
    tile_expert = jnp.minimum(jnp.sum((tile_start[:, None] >= ends[None, :]).astype(jnp.int32), axis=1),
                              n_experts - 1).astype(jnp.int32)
    tile_valid = (tile_start < ends[-1]).astype(jnp.int32)

    xs = jnp.take(h, row_src, axis=0)
    ys = _moe_experts(xs, tile_expert, tile_valid, wg, wu, wd)
    pos2 = pos.reshape(n, TOP_K)
    y0 = jnp.take(ys, pos2[:, 0], axis=0)
    y1 = jnp.take(ys, pos2[:, 1], axis=0)
    return _combine(x2, top_w, y0, y1)


def kernel(x, attn_norm, w_in, b_gate, q_norm, k_norm, sinks, conv_w, conv_b, lru_w_r, lru_b_r, lru_w_i, lru_b_i, lru_lambda, w_proj_a, w_proj_b, w_proj_c, w_out, ffn_norm, w_ffn_gate, w_ffn_up, w_ffn_down, w_router, w_exp_gate, w_exp_up, w_exp_down):
    batch, seq, d = x.shape
    depth = w_in.shape[0]
    swa_q = w_proj_a.shape[1]
    sb_w = w_proj_b.shape[1]
    lru_w = w_proj_c.shape[1]
    swa_kv = SWA_KV_HEADS * HEAD_DIM
    widths = (swa_q, swa_kv, swa_kv, sb_w, sb_w, sb_w, lru_w, lru_w, N_BRANCHES * d)
    assert sum(widths) == w_in.shape[2]
    assert seq % SB_QBLK == 0 and seq % WINDOW == 0

    x2 = x.reshape(batch * seq, d).astype(F32)
    for l in range(depth):
        gain = attn_norm[l].astype(F32).reshape(1, d)
        qa, ka, va, qb, kb, vb, xc, gc, gates = _in_proj(x2, gain, w_in[l].astype(BF16), widths)
        o_a = _swa(qa, ka, va, q_norm[l], k_norm[l], sinks[l], batch, seq)
        o_b = _stick_breaking(qb, kb, vb, batch, seq)
        o_c = _rglru(xc, gc, conv_w[l], conv_b[l], lru_w_r[l], lru_b_r[l], lru_w_i[l], lru_b_i[l],
                     lru_lambda[l], batch, seq)
        x2 = _merge(x2, o_a, o_b, o_c, gates, b_gate[l],
                    w_proj_a[l].astype(BF16), w_proj_b[l].astype(BF16), w_proj_c[l].astype(BF16),
                    w_out[l].astype(BF16))
        fgain = ffn_norm[l].astype(F32).reshape(1, d)
        j = l // 2
        if l % 2 == 0:
            x2 = _dense_ffn(x2, fgain, w_ffn_gate[j].astype(BF16), w_ffn_up[j].astype(BF16),
                            w_ffn_down[j].astype(BF16))
        else:
            x2 = _moe_ffn(x2, fgain, w_router[j], w_exp_gate[j].astype(BF16), w_exp_up[j].astype(BF16),
                          w_exp_down[j].astype(BF16))
    return x2.reshape(batch, seq, d).astype(x.dtype)
```

```python
import functools

import numpy as np
import jax
import jax.numpy as jnp
from jax import lax
from jax.experimental import pallas as pl
from jax.experimental.pallas import tpu as pltpu

F32 = jnp.float32
BF16 = jnp.bfloat16

HEAD_DIM = 64
SWA_KV_HEADS = 2
WINDOW = 128
LRU_HEADS = 8
LRU_C = 8.0
CONV_W = 4
N_BRANCHES = 3
TOP_K = 2
EPS = 1e-6

LANES = 128
SUBLANES = 8
VMEM_LIMIT = 56 * 1024 * 1024
NEG_BIG = -1e30

SB_QBLK = 256
SB_UNDERFLOW = 105.0
LRU_CHUNK = 512
ROW_TILE = 512
MOE_TILE = 512


def _cparams(*sem):
    return pltpu.CompilerParams(dimension_semantics=sem, vmem_limit_bytes=VMEM_LIMIT)


def _resident(shape):
    nd = len(shape)
    return pl.BlockSpec(shape, lambda *_: (0,) * nd, pipeline_mode=pl.Buffered(1))


def _dot(a, b):
    return jnp.dot(a, b, preferred_element_type=F32)


def _dot_nt(a, b):
    return lax.dot_general(a, b, (((1,), (1,)), ((), ())), preferred_element_type=F32)


def _split_dot(a, b_bf16):
    hi = a.astype(BF16)
    lo = (a - hi.astype(F32)).astype(BF16)
    return _dot(hi, b_bf16) + _dot(lo, b_bf16)


def _rms(x, gain):
    ms = jnp.mean(x * x, axis=-1, keepdims=True)
    return x * lax.rsqrt(ms + EPS) * gain


CAST_BLOCK_BYTES = 4 * 1024 * 1024


def _cast_kernel(x_ref, o_ref):
    o_ref[...] = x_ref[...].astype(o_ref.dtype)


def _to_bf16(w, layer):
    shape = w.shape[1:]
    cols = shape[-1]
    rows = int(np.prod(shape[:-1]))
    tr = rows
    while tr * cols * 4 > CAST_BLOCK_BYTES and tr % 32 == 0:
        tr //= 2
    out = pl.pallas_call(
        _cast_kernel,
        grid=(rows // tr,),
        in_specs=[pl.BlockSpec((None, tr, cols), lambda i: (layer, i, 0))],
        out_specs=pl.BlockSpec((tr, cols), lambda i: (i, 0)),
        out_shape=jax.ShapeDtypeStruct((rows, cols), BF16),
        compiler_params=_cparams("parallel"),
        name="cast_bf16",
    )(w.reshape(w.shape[0], rows, cols))
    return out.reshape(shape)


def _head_norm(xf, bd, gain):
    ms = _dot((xf * xf).astype(BF16), bd)
    return xf * lax.rsqrt(ms + EPS) * gain


def _in_proj_kernel(widths, x_ref, g_ref, w_ref, bdq_ref, bdk_ref, qg_ref, kg_ref, *out_refs):
    h = _rms(x_ref[...], g_ref[...]).astype(BF16)
    head_norms = ((bdq_ref, qg_ref), (bdk_ref, kg_ref))
    off = 0
    for idx, (ref, width) in enumerate(zip(out_refs, widths)):
        if idx < len(head_norms):
            bd_ref, gain_ref = head_norms[idx]
            acc = _dot(h, w_ref[:, off:off + width])
            ref[...] = _head_norm(acc, bd_ref[...], gain_ref[...]).astype(ref.dtype)
        else:
            for c in range(0, width, 1024):
                cw = min(1024, width - c)
                ref[:, c:c + cw] = _dot(h, w_ref[:, off + c:off + c + cw]).astype(ref.dtype)
        off += width


def _block_diag_mean(width):
    idx = np.arange(width) // HEAD_DIM
    return jnp.asarray((idx[:, None] == idx[None, :]).astype(np.float32) / HEAD_DIM, BF16)


def _in_proj(x2, gain, w_bf16, widths, q_norm, k_norm):
    n, d = x2.shape
    tm = min(ROW_TILE, n)
    qw, kw = widths[0], widths[1]
    qg = (jnp.tile(q_norm.astype(F32), qw // HEAD_DIM) * (HEAD_DIM ** -0.5)).reshape(1, qw)
    kg = jnp.tile(k_norm.astype(F32), kw // HEAD_DIM).reshape(1, kw)
    return pl.pallas_call(
        functools.partial(_in_proj_kernel, widths),
        grid=(n // tm,),
        in_specs=[pl.BlockSpec((tm, d), lambda i: (i, 0)),
                  _resident((1, d)),
                  _resident(w_bf16.shape),
                  _resident((qw, qw)), _resident((kw, kw)),
                  _resident((1, qw)), _resident((1, kw))],
        out_specs=[pl.BlockSpec((tm, w), lambda i: (i, 0)) for w in widths],
        out_shape=[jax.ShapeDtypeStruct((n, w), BF16) for w in widths],
        compiler_params=_cparams("parallel"),
        name="in_proj",
    )(x2, gain, w_bf16, _block_diag_mean(qw), _block_diag_mean(kw), qg, kg)


def _swa_kernel(n_heads, q_ref, kc_ref, kp_ref, vc_ref, vp_ref, sink_ref, o_ref):
    n = pl.program_id(1)
    blk = q_ref.shape[0]
    group = n_heads // SWA_KV_HEADS
    qn = q_ref[...]
    kn = jnp.concatenate([kp_ref[...], kc_ref[...]], axis=0)
    v = jnp.concatenate([vp_ref[...], vc_ref[...]], axis=0)
    k_var = (kn, pltpu.roll(kn, HEAD_DIM, 1))
    v_var = (v, pltpu.roll(v, HEAD_DIM, 1))

    qpos = lax.broadcasted_iota(jnp.int32, (blk, 2 * blk), 0) + blk
    kpos = lax.broadcasted_iota(jnp.int32, (blk, 2 * blk), 1)
    rel = qpos - kpos
    mask = (rel >= 0) & (rel < WINDOW) & ((n > 0) | (kpos >= blk))
    lane = lax.broadcasted_iota(jnp.int32, (blk, LANES), 1)
    low = lane < HEAD_DIM

    for pair in range(n_heads // 2):
        q_pair = qn[:, pair * LANES:(pair + 1) * LANES]
        outs = []
        for parity in range(2):
            j = 2 * pair + parity
            h = j // group
            qm = jnp.where(low if parity == 0 else ~low, q_pair, jnp.zeros_like(q_pair))
            s = _dot_nt(qm, k_var[(h + parity) % 2])
            s = jnp.where(mask, s, NEG_BIG)
            sink = sink_ref[j]
            m = jnp.maximum(jnp.max(s, axis=-1, keepdims=True), sink)
            p = jnp.exp(s - m)
            denom = jnp.sum(p, axis=-1, keepdims=True) + jnp.exp(sink - m)
            o = _dot(p.astype(BF16), v_var[(h + parity) % 2])
            outs.append(o / denom)
        o_ref[:, pair * LANES:(pair + 1) * LANES] = jnp.where(low, outs[0], outs[1]).astype(o_ref.dtype)


def _swa(qa, ka, va, sinks, batch, seq):
    n, qw = qa.shape
    kw = ka.shape[1]
    n_heads = qw // HEAD_DIM
    blk = WINDOW
    nb = seq // blk
    cur = lambda b, i: (b * nb + i, 0)
    prev = lambda b, i: (b * nb + jnp.maximum(i - 1, 0), 0)
    return pl.pallas_call(
        functools.partial(_swa_kernel, n_heads),
        grid=(batch, nb),
        in_specs=[pl.BlockSpec((blk, qw), cur),
                  pl.BlockSpec((blk, kw), cur), pl.BlockSpec((blk, kw), prev),
                  pl.BlockSpec((blk, kw), cur), pl.BlockSpec((blk, kw), prev),
                  pl.BlockSpec(memory_space=pltpu.SMEM)],
        out_specs=pl.BlockSpec((blk, qw), cur),
        out_shape=jax.ShapeDtypeStruct((n, qw), BF16),
        compiler_params=_cparams("parallel", "parallel"),
        name="swa",
    )(qa, ka, ka, va, va, sinks.astype(F32))


def _sb_kernel(q_ref, k_ref, v_ref, tri_ref, o_ref):
    i = pl.program_id(2)
    qb = q_ref.shape[0]
    q = q_ref[...]
    lane = lax.broadcasted_iota(jnp.int32, (qb, LANES), 1)
    low = lane < HEAD_DIM
    zero = jnp.zeros_like(q)
    scale = jnp.asarray(HEAD_DIM ** -0.5, q.dtype)
    q_heads = (jnp.where(low, q, zero) * scale, jnp.where(low, zero, q) * scale)
    tri = tri_ref[...]
    strict = (lax.broadcasted_iota(jnp.int32, (qb, qb), 1)
              < lax.broadcasted_iota(jnp.int32, (qb, qb), 0))

    def head_chunk(qh, k, v, carry, diagonal):
        z = _dot_nt(qh, k)
        sp = jnp.maximum(z, 0.0) + jnp.log(1.0 + jnp.exp(-jnp.abs(z)))
        spm = jnp.where(strict, sp, 0.0) if diagonal else sp
        later = _dot(spm.astype(BF16), tri)
        w = jnp.exp(z - sp - (later + carry))
        if diagonal:
            w = jnp.where(strict, w, 0.0)
        return _dot(w.astype(BF16), v), jnp.sum(spm, axis=-1, keepdims=True)

    def step(chunks, carries, accs):
        carries, accs = list(carries), list(accs)
        for c, diagonal in chunks:
            start = pl.multiple_of(c * qb, qb)
            k = k_ref[pl.ds(start, qb), :]
            v = v_ref[pl.ds(start, qb), :]
            for h in range(2):
                pv, rowsum = head_chunk(q_heads[h], k, v, carries[h], diagonal)
                accs[h] = accs[h] + pv
                carries[h] = carries[h] + rowsum
        return tuple(carries), tuple(accs)

    carries = (jnp.zeros((qb, 1), F32),) * 2
    accs = (jnp.zeros((qb, LANES), F32),) * 2
    carries, accs = lax.cond(i == 0,
                             lambda: step([(i, True)], carries, accs),
                             lambda: step([(i, True), (i - 1, False)], carries, accs))

    def cond(state):
        t, (c0, c1), _ = state
        return (t < i - 1) & (jnp.minimum(jnp.min(c0), jnp.min(c1)) <= SB_UNDERFLOW)

    def body(state):
        t, carries, accs = state
        carries, accs = step([(i - 2 - t, False)], carries, accs)
        return t + 1, carries, accs

    _, _, accs = lax.while_loop(cond, body, (jnp.int32(0), carries, accs))
    o_ref[...] = jnp.where(low, accs[0], accs[1]).astype(o_ref.dtype)


def _stick_breaking(qb, kb, vb, batch, seq):
    n, w = qb.shape
    blk = min(SB_QBLK, seq)
    nq = seq // blk
    tri = jnp.asarray(np.tril(np.ones((blk, blk), np.float32), -1), BF16)
    return pl.pallas_call(
        _sb_kernel,
        grid=(batch, w // LANES, nq),
        in_specs=[pl.BlockSpec((blk, LANES), lambda b, p, i: (b * nq + i, p)),
                  pl.BlockSpec((seq, LANES), lambda b, p, i: (b, p)),
                  pl.BlockSpec((seq, LANES), lambda b, p, i: (b, p)),
                  _resident((blk, blk))],
        out_specs=pl.BlockSpec((blk, LANES), lambda b, p, i: (b * nq + i, p)),
        out_shape=jax.ShapeDtypeStruct((n, w), BF16),
        compiler_params=_cparams("parallel", "parallel", "arbitrary"),
        name="stick_breaking",
    )(qb, kb, vb, tri)


def _gelu_tanh(x):
    return 0.5 * x * (1.0 + jnp.tanh(np.sqrt(2.0 / np.pi) * (x + 0.044715 * (x * x * x))))


def _lru_kernel(x_ref, g_ref, cw_ref, cb_ref, wri_ref, bri_ref, lam_ref, o_ref, tail_ref, h_ref):
    j = pl.program_id(1)
    ts, width = x_ref.shape

    @pl.when(j == 0)
    def _():
        tail_ref[...] = jnp.zeros_like(tail_ref)
        h_ref[...] = jnp.zeros_like(h_ref)

    x = x_ref[...].astype(F32)
    tail = tail_ref[...]
    row8 = lax.broadcasted_iota(jnp.int32, (SUBLANES, width), 0)
    y = x * cw_ref[CONV_W - 1:CONV_W, :] + cb_ref[...]
    for d in range(1, CONV_W):
        xs = pltpu.roll(x, d, 0)
        top = jnp.where(row8 < d, pltpu.roll(tail, d, 0), xs[:SUBLANES])
        xs = jnp.concatenate([top, xs[SUBLANES:]], axis=0)
        y = y + xs * cw_ref[CONV_W - 1 - d:CONV_W - d, :]
    tail_ref[...] = x[ts - SUBLANES:]

    ri = _dot(y.astype(BF16), wri_ref[...]) + bri_ref[...]
    r = jax.nn.sigmoid(ri[:, :width])
    gate_i = jax.nn.sigmoid(ri[:, width:])
    lam = lam_ref[...]
    softplus_neg_lam = jnp.maximum(-lam, 0.0) + jnp.log(1.0 + jnp.exp(-jnp.abs(lam)))
    log_a = (-LRU_C) * r * softplus_neg_lam
    a = jnp.exp(log_a)
    u = jnp.sqrt(1.0 - a * a) * (gate_i * y)

    in_group = jnp.bitwise_and(lax.broadcasted_iota(jnp.int32, (ts, width), 0), SUBLANES - 1)
    d = 1
    while d < SUBLANES:
        keep = in_group >= d
        a_sh = jnp.where(keep, pltpu.roll(a, d, 0), 1.0)
        u_sh = jnp.where(keep, pltpu.roll(u, d, 0), 0.0)
        u = a * u_sh + u
        a = a * a_sh
        d *= 2
    state = h_ref[...]
    groups = []
    for g in range(ts // SUBLANES):
        rows = slice(g * SUBLANES, (g + 1) * SUBLANES)
        hg = u[rows] + a[rows] * state
        groups.append(hg)
        state = hg[SUBLANES - 1:SUBLANES, :]
    h = jnp.concatenate(groups, axis=0)
    h_ref[...] = state
    o_ref[...] = (h * _gelu_tanh(g_ref[...].astype(F32))).astype(o_ref.dtype)


def _block_diag(w):
    heads, blk, _ = w.shape
    eye = jnp.eye(heads, dtype=w.dtype)
    return jnp.einsum('hij,hg->higj', w, eye).reshape(heads * blk, heads * blk)


def _rglru(xc, gc, conv_w, conv_b, w_r, b_r, w_i, b_i, lam, batch, seq):
    n, width = xc.shape
    ts = min(LRU_CHUNK, seq)
    nc = seq // ts
    wri = jnp.concatenate([_block_diag(w_r), _block_diag(w_i)], axis=1).astype(BF16)
    bri = jnp.concatenate([b_r, b_i]).astype(F32).reshape(1, 2 * width)
    blk = lambda b, j: (b * nc + j, 0)
    return pl.pallas_call(
        _lru_kernel,
        grid=(batch, nc),
        in_specs=[pl.BlockSpec((ts, width), blk), pl.BlockSpec((ts, width), blk),
                  _resident((CONV_W, width)), _resident((1, width)),
                  _resident((width, 2 * width)), _resident((1, 2 * width)),
                  _resident((1, width))],
        out_specs=pl.BlockSpec((ts, width), blk),
        out_shape=jax.ShapeDtypeStruct((n, width), BF16),
        scratch_shapes=[pltpu.VMEM((SUBLANES, width), F32), pltpu.VMEM((1, width), F32)],
        compiler_params=_cparams("parallel", "arbitrary"),
        name="rglru",
    )(xc, gc, conv_w.astype(F32), conv_b.astype(F32).reshape(1, width), wri, bri,
      lam.astype(F32).reshape(1, width))


def _merge_kernel(x_ref, oa_ref, ob_ref, oc_ref, gt_ref, bg_ref, wa_ref, wb_ref, wc_ref, wo_ref, o_ref):
    d = x_ref.shape[1]
    merged = None
    for idx, (b_ref, w_ref) in enumerate(((oa_ref, wa_ref), (ob_ref, wb_ref), (oc_ref, wc_ref))):
        gate = jax.nn.sigmoid(gt_ref[:, idx * d:(idx + 1) * d].astype(F32) + bg_ref[:, idx * d:(idx + 1) * d])
        term = gate * _dot(b_ref[...], w_ref[...])
        merged = term if merged is None else merged + term
    o_ref[...] = x_ref[...] + _dot(merged.astype(BF16), wo_ref[...])


def _merge(x2, o_a, o_b, o_c, gates, b_gate, wa, wb, wc, wo):
    n, d = x2.shape
    tm = min(ROW_TILE, n)
    row = lambda i: (i, 0)
    return pl.pallas_call(
        _merge_kernel,
        grid=(n // tm,),
        in_specs=[pl.BlockSpec((tm, d), row),
                  pl.BlockSpec((tm, o_a.shape[1]), row), pl.BlockSpec((tm, o_b.shape[1]), row),
                  pl.BlockSpec((tm, o_c.shape[1]), row), pl.BlockSpec((tm, gates.shape[1]), row),
                  _resident((1, gates.shape[1])),
                  _resident(wa.shape), _resident(wb.shape), _resident(wc.shape), _resident(wo.shape)],
        out_specs=pl.BlockSpec((tm, d), row),
        out_shape=jax.ShapeDtypeStruct((n, d), F32),
        compiler_params=_cparams("parallel"),
        name="merge",
    )(x2, o_a, o_b, o_c, gates, b_gate.astype(F32).reshape(1, -1), wa, wb, wc, wo)


def _swiglu_tile(h, wg_ref, wu_ref, wd_ref, ff_chunk):
    d_ff = wg_ref.shape[-1]
    acc = None
    for c in range(0, d_ff, ff_chunk):
        g = _dot(h, wg_ref[:, c:c + ff_chunk])
        u = _dot(h, wu_ref[:, c:c + ff_chunk])
        act = (g * jax.nn.sigmoid(g) * u).astype(BF16)
        part = _dot(act, wd_ref[c:c + ff_chunk, :])
        acc = part if acc is None else acc + part
    return acc


def _ffn_chunk(d_ff):
    for c in (512, 896, 256, 128):
        if d_ff % c == 0:
            return c
    return d_ff


def _dense_ffn_kernel(ff_chunk, x_ref, g_ref, wg_ref, wu_ref, wd_ref, o_ref):
    x = x_ref[...]
    h = _rms(x, g_ref[...]).astype(BF16)
    o_ref[...] = x + _swiglu_tile(h, wg_ref, wu_ref, wd_ref, ff_chunk)


def _dense_ffn(x2, gain, wg, wu, wd):
    n, d = x2.shape
    tm = min(ROW_TILE, n)
    return pl.pallas_call(
        functools.partial(_dense_ffn_kernel, _ffn_chunk(wg.shape[1])),
        grid=(n // tm,),
        in_specs=[pl.BlockSpec((tm, d), lambda i: (i, 0)), _resident((1, d)),
                  _resident(wg.shape), _resident(wu.shape), _resident(wd.shape)],
        out_specs=pl.BlockSpec((tm, d), lambda i: (i, 0)),
        out_shape=jax.ShapeDtypeStruct((n, d), F32),
        compiler_params=_cparams("parallel"),
        name="dense_ffn",
    )(x2, gain, wg, wu, wd)


def _router_kernel(n_experts, x_ref, g_ref, wr_ref, tril_ref, h_ref, idx_ref, wt_ref, total_ref, count_ref):
    h = _rms(x_ref[...], g_ref[...])
    h_ref[...] = h.astype(h_ref.dtype)
    w = wr_ref[...]
    h1 = h.astype(BF16)
    h2 = (h - h1.astype(F32)).astype(BF16)
    h3 = (h - h1.astype(F32) - h2.astype(F32)).astype(BF16)
    w1 = w.astype(BF16)
    w2 = (w - w1.astype(F32)).astype(BF16)
    w3 = (w - w1.astype(F32) - w2.astype(F32)).astype(BF16)
    logits = (_dot(h1, w1) + (_dot(h1, w2) + _dot(h2, w1))
              + (_dot(h2, w2) + _dot(h1, w3) + _dot(h3, w1)))
    lane = lax.broadcasted_iota(jnp.int32, logits.shape, 1)
    logits = jnp.where(lane < n_experts, logits, NEG_BIG)
    m1 = jnp.max(logits, axis=-1, keepdims=True)
    i1 = jnp.min(jnp.where(logits == m1, lane, LANES), axis=-1, keepdims=True)
    rest = jnp.where(lane == i1, NEG_BIG, logits)
    m2 = jnp.max(rest, axis=-1, keepdims=True)
    i2 = jnp.min(jnp.where(rest == m2, lane, LANES), axis=-1, keepdims=True)
    e2 = jnp.exp(m2 - m1)
    wt1 = 1.0 / (1.0 + e2)
    wt2 = e2 / (1.0 + e2)
    wt_ref[...] = jnp.where(lane == 0, wt1, jnp.where(lane == 1, wt2, 0.0))

    @pl.when(pl.program_id(0) == 0)
    def _():
        count_ref[...] = jnp.zeros_like(count_ref)

    chosen = jnp.where((lane == i1) | (lane == i2), 1.0, 0.0)
    before = _dot(tril_ref[...], chosen.astype(BF16)) + count_ref[...]
    r1 = jnp.sum(jnp.where(lane == i1, before, 0.0), axis=-1, keepdims=True).astype(jnp.int32)
    r2 = jnp.sum(jnp.where(lane == i2, before, 0.0), axis=-1, keepdims=True).astype(jnp.int32)
    count_ref[...] = count_ref[...] + jnp.sum(chosen, axis=0, keepdims=True)
    total_ref[...] = count_ref[...]
    idx_ref[...] = jnp.where(lane == 0, i1, jnp.where(lane == 1, i2,
                             jnp.where(lane == 2, r1, jnp.where(lane == 3, r2, 0))))


def _router(x2, gain, w_router):
    n, d = x2.shape
    n_experts = w_router.shape[1]
    tm = min(ROW_TILE, n)
    wr = jnp.zeros((d, LANES), F32).at[:, :n_experts].set(w_router.astype(F32))
    tril = jnp.asarray(np.tril(np.ones((tm, tm), np.float32), -1), BF16)
    row = lambda i: (i, 0)
    return pl.pallas_call(
        functools.partial(_router_kernel, n_experts),
        grid=(n // tm,),
        in_specs=[pl.BlockSpec((tm, d), row), _resident((1, d)), _resident((d, LANES)), _resident((tm, tm))],
        out_specs=[pl.BlockSpec((tm, d), row), pl.BlockSpec((tm, LANES), row), pl.BlockSpec((tm, LANES), row),
                   pl.BlockSpec((1, LANES), lambda i: (0, 0))],
        out_shape=[jax.ShapeDtypeStruct((n, d), BF16), jax.ShapeDtypeStruct((n, LANES), jnp.int32),
                   jax.ShapeDtypeStruct((n, LANES), F32), jax.ShapeDtypeStruct((1, LANES), F32)],
        scratch_shapes=[pltpu.VMEM((1, LANES), F32)],
        compiler_params=_cparams("arbitrary"),
        name="router",
    )(x2, gain, wr, tril)


def _moe_kernel(ff_chunk, te_ref, tv_ref, x_ref, wg_ref, wu_ref, wd_ref, o_ref):
    t = pl.program_id(0)

    @pl.when(tv_ref[t] > 0)
    def _():
        y = _swiglu_tile(x_ref[...], wg_ref.at[0], wu_ref.at[0], wd_ref.at[0], ff_chunk)
        o_ref[...] = y.astype(o_ref.dtype)

    @pl.when(tv_ref[t] == 0)
    def _():
        o_ref[...] = jnp.zeros_like(o_ref)


def _moe_experts(xs, tile_expert, tile_valid, wg, wu, wd):
    p, d = xs.shape
    d_ff = wg.shape[2]
    tm = MOE_TILE
    single = pl.Buffered(1)
    grid_spec = pltpu.PrefetchScalarGridSpec(
        num_scalar_prefetch=2,
        grid=(p // tm,),
        in_specs=[pl.BlockSpec((tm, d), lambda t, te, tv: (t, 0)),
                  pl.BlockSpec((1, d, d_ff), lambda t, te, tv: (te[t], 0, 0), pipeline_mode=single),
                  pl.BlockSpec((1, d, d_ff), lambda t, te, tv: (te[t], 0, 0), pipeline_mode=single),
                  pl.BlockSpec((1, d_ff, d), lambda t, te, tv: (te[t], 0, 0), pipeline_mode=single)],
        out_specs=pl.BlockSpec((tm, d), lambda t, te, tv: (t, 0)),
    )
    return pl.pallas_call(
        functools.partial(_moe_kernel, _ffn_chunk(d_ff)),
        grid_spec=grid_spec,
        out_shape=jax.ShapeDtypeStruct((p, d), BF16),
        compiler_params=_cparams("arbitrary"),
        name="moe_experts",
    )(tile_expert, tile_valid, xs, wg, wu, wd)


def _combine_kernel(x_ref, wt_ref, y0_ref, y1_ref, o_ref):
    wt = wt_ref[...]
    o_ref[...] = x_ref[...] + (wt[:, 0:1] * y0_ref[...].astype(F32) + wt[:, 1:2] * y1_ref[...].astype(F32))


def _combine(x2, wt, y0, y1):
    n, d = x2.shape
    tm = min(ROW_TILE, n)
    row = lambda i: (i, 0)
    return pl.pallas_call(
        _combine_kernel,
        grid=(n // tm,),
        in_specs=[pl.BlockSpec((tm, d), row), pl.BlockSpec((tm, LANES), row),
                  pl.BlockSpec((tm, d), row), pl.BlockSpec((tm, d), row)],
        out_specs=pl.BlockSpec((tm, d), row),
        out_shape=jax.ShapeDtypeStruct((n, d), F32),
        compiler_params=_cparams("parallel"),
        name="moe_combine",
    )(x2, wt, y0, y1)


def _moe_ffn(x2, gain, w_router, wg, wu, wd):
    n, d = x2.shape
    n_experts = w_router.shape[1]
    tm = MOE_TILE
    h, route, top_w, totals = _router(x2, gain, w_router)

    sizes = totals[0, :n_experts].astype(jnp.int32)
    padded = ((sizes + tm - 1) // tm) * tm
    starts = jnp.cumsum(padded) - padded
    expert = route[:, :TOP_K]
    rank = route[:, TOP_K:2 * TOP_K]
    is_e = expert[:, :, None] == jnp.arange(n_experts, dtype=jnp.int32)[None, None, :]
    pos2 = rank + jnp.sum(jnp.where(is_e, starts[None, None, :], 0), axis=-1)
    n_tiles = (n * TOP_K) // tm + n_experts
    p = n_tiles * tm
    token = jnp.arange(n * TOP_K, dtype=jnp.int32) // TOP_K
    row_src = jnp.zeros((p,), jnp.int32).at[pos2.reshape(-1)].set(
        token, unique_indices=True, mode="promise_in_bounds")
    tile_start = jnp.arange(n_tiles, dtype=jnp.int32) * tm
    ends = starts + padded
    tile_expert = jnp.minimum(jnp.sum((tile_start[:, None] >= ends[None, :]).astype(jnp.int32), axis=1),
                              n_experts - 1).astype(jnp.int32)
    tile_valid = (tile_start < ends[-1]).astype(jnp.int32)

    xs = jnp.take(h, row_src, axis=0)
    ys = _moe_experts(xs, tile_expert, tile_valid, wg, wu, wd)
    y0 = jnp.take(ys, pos2[:, 0], axis=0)
    y1 = jnp.take(ys, pos2[:, 1], axis=0)
    return _combine(x2, top_w, y0, y1)


def kernel(x, attn_norm, w_in, b_gate, q_norm, k_norm, sinks, conv_w, conv_b, lru_w_r, lru_b_r, lru_w_i, lru_b_i, lru_lambda, w_proj_a, w_proj_b, w_proj_c, w_out, ffn_norm, w_ffn_gate, w_ffn_up, w_ffn_down, w_router, w_exp_gate, w_exp_up, w_exp_down):
    batch, seq, d = x.shape
    depth = w_in.shape[0]
    swa_q = w_proj_a.shape[1]
    sb_w = w_proj_b.shape[1]
    lru_w = w_proj_c.shape[1]
    swa_kv = SWA_KV_HEADS * HEAD_DIM
    widths = (swa_q, swa_kv, swa_kv, sb_w, sb_w, sb_w, lru_w, lru_w, N_BRANCHES * d)
    assert sum(widths) == w_in.shape[2]
    assert seq % SB_QBLK == 0 and seq % WINDOW == 0

    x2 = x.reshape(batch * seq, d).astype(F32)
    for l in range(depth):
        gain = attn_norm[l].astype(F32).reshape(1, d)
        qa, ka, va, qb, kb, vb, xc, gc, gates = _in_proj(x2, gain, _to_bf16(w_in, l), widths,
                                                         q_norm[l], k_norm[l])
        o_a = _swa(qa, ka, va, sinks[l], batch, seq)
        o_b = _stick_breaking(qb, kb, vb, batch, seq)
        o_c = _rglru(xc, gc, conv_w[l], conv_b[l], lru_w_r[l], lru_b_r[l], lru_w_i[l], lru_b_i[l],
                     lru_lambda[l], batch, seq)
        x2 = _merge(x2, o_a, o_b, o_c, gates, b_gate[l],
                    _to_bf16(w_proj_a, l), _to_bf16(w_proj_b, l), _to_bf16(w_proj_c, l),
                    _to_bf16(w_out, l))
        fgain = ffn_norm[l].astype(F32).reshape(1, d)
        j = l // 2
        if l % 2 == 0:
            x2 = _dense_ffn(x2, fgain, _to_bf16(w_ffn_gate, j), _to_bf16(w_ffn_up, j),
                            _to_bf16(w_ffn_down, j))
        else:
            x2 = _moe_ffn(x2, fgain, w_router[j], _to_bf16(w_exp_gate, j), _to_bf16(w_exp_up, j),
                          _to_bf16(w_exp_down, j))
    return x2.reshape(batch, seq, d).astype(x.dtype)
```

```python
import functools

import numpy as np
import jax
import jax.numpy as jnp
from jax import lax
from jax.experimental import pallas as pl
from jax.experimental.pallas import tpu as pltpu

F32 = jnp.float32
BF16 = jnp.bfloat16

HEAD_DIM = 64
SWA_KV_HEADS = 2
WINDOW = 128
LRU_HEADS = 8
LRU_C = 8.0
CONV_W = 4
N_BRANCHES = 3
TOP_K = 2
EPS = 1e-6

LANES = 128
SUBLANES = 8
VMEM_LIMIT = 56 * 1024 * 1024
NEG_BIG = -1e30

SB_QBLK = 256
SB_UNDERFLOW = 105.0
LRU_CHUNK = 512
ROW_TILE = 512
MOE_TILE = 512


def _cparams(*sem):
    return pltpu.CompilerParams(dimension_semantics=sem, vmem_limit_bytes=VMEM_LIMIT)


def _resident(shape):
    nd = len(shape)
    return pl.BlockSpec(shape, lambda *_: (0,) * nd, pipeline_mode=pl.Buffered(1))


def _dot(a, b):
    return jnp.dot(a, b, preferred_element_type=F32)


def _dot_nt(a, b):
    return lax.dot_general(a, b, (((1,), (1,)), ((), ())), preferred_element_type=F32)


def _split_dot(a, b_bf16):
    hi = a.astype(BF16)
    lo = (a - hi.astype(F32)).astype(BF16)
    return _dot(hi, b_bf16) + _dot(lo, b_bf16)


def _rms(x, gain):
    ms = jnp.mean(x * x, axis=-1, keepdims=True)
    return x * lax.rsqrt(ms + EPS) * gain


CAST_BLOCK_BYTES = 4 * 1024 * 1024


def _cast_kernel(x_ref, o_ref):
    o_ref[...] = x_ref[...].astype(o_ref.dtype)


def _to_bf16(w, layer):
    shape = w.shape[1:]
    cols = shape[-1]
    rows = int(np.prod(shape[:-1]))
    tr = rows
    while tr * cols * 4 > CAST_BLOCK_BYTES and tr % 32 == 0:
        tr //= 2
    out = pl.pallas_call(
        _cast_kernel,
        grid=(rows // tr,),
        in_specs=[pl.BlockSpec((None, tr, cols), lambda i: (layer, i, 0))],
        out_specs=pl.BlockSpec((tr, cols), lambda i: (i, 0)),
        out_shape=jax.ShapeDtypeStruct((rows, cols), BF16),
        compiler_params=_cparams("parallel"),
        name="cast_bf16",
    )(w.reshape(w.shape[0], rows, cols))
    return out.reshape(shape)


def _head_norm(xf, bd, gain):
    ms = _dot((xf * xf).astype(BF16), bd)
    return xf * lax.rsqrt(ms + EPS) * gain


def _in_proj_kernel(widths, x_ref, g_ref, w_ref, bdq_ref, bdk_ref, qg_ref, kg_ref, *out_refs):
    h = _rms(x_ref[...], g_ref[...]).astype(BF16)
    normed = ((bdq_ref, qg_ref), (bdk_ref, kg_ref))
    offs = [sum(widths[:i]) for i in range(len(widths))]
    accs = [_dot(h, w_ref[:, offs[i]:offs[i] + widths[i]]) for i in range(len(normed))]
    for idx in range(len(normed), len(widths)):
        ref, width, off = out_refs[idx], widths[idx], offs[idx]
        for c in range(0, width, 1024):
            cw = min(1024, width - c)
            ref[:, c:c + cw] = _dot(h, w_ref[:, off + c:off + c + cw]).astype(ref.dtype)
    for idx, (bd_ref, gain_ref) in enumerate(normed):
        out_refs[idx][...] = _head_norm(accs[idx], bd_ref[...], gain_ref[...]).astype(out_refs[idx].dtype)


def _block_diag_mean(width):
    idx = np.arange(width) // HEAD_DIM
    return jnp.asarray((idx[:, None] == idx[None, :]).astype(np.float32) / HEAD_DIM, BF16)


def _in_proj(x2, gain, w_bf16, widths, q_norm, k_norm):
    n, d = x2.shape
    tm = min(ROW_TILE, n)
    qw, kw = widths[0], widths[1]
    qg = (jnp.tile(q_norm.astype(F32), qw // HEAD_DIM) * (HEAD_DIM ** -0.5)).reshape(1, qw)
    kg = jnp.tile(k_norm.astype(F32), kw // HEAD_DIM).reshape(1, kw)
    return pl.pallas_call(
        functools.partial(_in_proj_kernel, widths),
        grid=(n // tm,),
        in_specs=[pl.BlockSpec((tm, d), lambda i: (i, 0)),
                  _resident((1, d)),
                  _resident(w_bf16.shape),
                  _resident((qw, qw)), _resident((kw, kw)),
                  _resident((1, qw)), _resident((1, kw))],
        out_specs=[pl.BlockSpec((tm, w), lambda i: (i, 0)) for w in widths],
        out_shape=[jax.ShapeDtypeStruct((n, w), BF16) for w in widths],
        compiler_params=_cparams("parallel"),
        name="in_proj",
    )(x2, gain, w_bf16, _block_diag_mean(qw), _block_diag_mean(kw), qg, kg)


def _swa_kernel(n_heads, q_ref, kc_ref, kp_ref, vc_ref, vp_ref, sink_ref, o_ref):
    n = pl.program_id(1)
    blk = q_ref.shape[0]
    group = n_heads // SWA_KV_HEADS
    qn = q_ref[...]
    kc, kp, vc, vp = kc_ref[...], kp_ref[...], vc_ref[...], vp_ref[...]
    kc_var = tuple(jnp.transpose(t) for t in (kc, pltpu.roll(kc, HEAD_DIM, 1)))
    kp_var = tuple(jnp.transpose(t) for t in (kp, pltpu.roll(kp, HEAD_DIM, 1)))
    vc_var, vp_var = (vc, pltpu.roll(vc, HEAD_DIM, 1)), (vp, pltpu.roll(vp, HEAD_DIM, 1))

    upper = (lax.broadcasted_iota(jnp.int32, (blk, blk), 1)
             > lax.broadcasted_iota(jnp.int32, (blk, blk), 0))
    no_prev = upper & (n == 0)
    lane = lax.broadcasted_iota(jnp.int32, (blk, LANES), 1)
    low = lane < HEAD_DIM

    heads = range(n_heads)
    sel = [(j // group + j % 2) % 2 for j in heads]
    s_prev, s_cur = [], []
    for j in heads:
        q_pair = qn[:, (j // 2) * LANES:(j // 2 + 1) * LANES]
        qm = jnp.where(low if j % 2 == 0 else ~low, q_pair, jnp.zeros_like(q_pair))
        s_prev.append(_dot(qm, kp_var[sel[j]]))
        s_cur.append(_dot(qm, kc_var[sel[j]]))
    p_prev, p_cur, denom = [], [], []
    for j in heads:
        s = jnp.where(no_prev, NEG_BIG, jnp.where(upper, s_prev[j], s_cur[j]))
        sink = sink_ref[j]
        m = jnp.maximum(jnp.max(s, axis=-1, keepdims=True), sink)
        p = jnp.exp(s - m)
        denom.append(jnp.sum(p, axis=-1, keepdims=True) + jnp.exp(sink - m))
        p_prev.append(jnp.where(upper, p, 0.0).astype(BF16))
        p_cur.append(jnp.where(upper, 0.0, p).astype(BF16))
    o_prev = [_dot(p_prev[j], vp_var[sel[j]]) for j in heads]
    o_cur = [_dot(p_cur[j], vc_var[sel[j]]) for j in heads]
    outs = [(o_prev[j] + o_cur[j]) / denom[j] for j in heads]
    for pair in range(n_heads // 2):
        o_ref[:, pair * LANES:(pair + 1) * LANES] = jnp.where(
            low, outs[2 * pair], outs[2 * pair + 1]).astype(o_ref.dtype)


def _swa(qa, ka, va, sinks, batch, seq):
    n, qw = qa.shape
    kw = ka.shape[1]
    n_heads = qw // HEAD_DIM
    blk = WINDOW
    nb = seq // blk
    cur = lambda b, i: (b * nb + i, 0)
    prev = lambda b, i: (b * nb + jnp.maximum(i - 1, 0), 0)
    return pl.pallas_call(
        functools.partial(_swa_kernel, n_heads),
        grid=(batch, nb),
        in_specs=[pl.BlockSpec((blk, qw), cur),
                  pl.BlockSpec((blk, kw), cur), pl.BlockSpec((blk, kw), prev),
                  pl.BlockSpec((blk, kw), cur), pl.BlockSpec((blk, kw), prev),
                  pl.BlockSpec(memory_space=pltpu.SMEM)],
        out_specs=pl.BlockSpec((blk, qw), cur),
        out_shape=jax.ShapeDtypeStruct((n, qw), BF16),
        compiler_params=_cparams("parallel", "parallel"),
        name="swa",
    )(qa, ka, ka, va, va, sinks.astype(F32))


def _sb_kernel(q_ref, k_ref, v_ref, tri_ref, o_ref):
    i = pl.program_id(2)
    qb = q_ref.shape[0]
    q = q_ref[...]
    lane = lax.broadcasted_iota(jnp.int32, (qb, LANES), 1)
    low = lane < HEAD_DIM
    zero = jnp.zeros_like(q)
    scale = jnp.asarray(HEAD_DIM ** -0.5, q.dtype)
    q_heads = (jnp.where(low, q, zero) * scale, jnp.where(low, zero, q) * scale)
    tri = tri_ref[...]
    strict = (lax.broadcasted_iota(jnp.int32, (qb, qb), 1)
              < lax.broadcasted_iota(jnp.int32, (qb, qb), 0))

    def step(chunks, carries, accs):
        carries, accs = list(carries), list(accs)
        jobs = [(ci, h) for ci in range(len(chunks)) for h in range(2)]
        ks, vs = [], []
        for c, _ in chunks:
            start = pl.multiple_of(c * qb, qb)
            ks.append(k_ref[pl.ds(start, qb), :])
            vs.append(v_ref[pl.ds(start, qb), :])
        z = {job: _dot_nt(q_heads[job[1]], ks[job[0]]) for job in jobs}
        sp, spm = {}, {}
        for job in jobs:
            zj = z[job]
            sp[job] = jnp.maximum(zj, 0.0) + jnp.log(1.0 + jnp.exp(-jnp.abs(zj)))
            spm[job] = jnp.where(strict, sp[job], 0.0) if chunks[job[0]][1] else sp[job]
        later = {job: _dot(spm[job].astype(BF16), tri) for job in jobs}
        w = {}
        for ci, h in jobs:
            wj = jnp.exp(z[ci, h] - sp[ci, h] - (later[ci, h] + carries[h]))
            w[ci, h] = (jnp.where(strict, wj, 0.0) if chunks[ci][1] else wj).astype(BF16)
            carries[h] = carries[h] + jnp.sum(spm[ci, h], axis=-1, keepdims=True)
        for ci, h in jobs:
            accs[h] = accs[h] + _dot(w[ci, h], vs[ci])
        return tuple(carries), tuple(accs)

    carries = (jnp.zeros((qb, 1), F32),) * 2
    accs = (jnp.zeros((qb, LANES), F32),) * 2
    carries, accs = lax.cond(i == 0,
                             lambda: step([(i, True)], carries, accs),
                             lambda: step([(i, True), (i - 1, False)], carries, accs))

    def cond(state):
        t, (c0, c1), _ = state
        return (t < i - 1) & (jnp.minimum(jnp.min(c0), jnp.min(c1)) <= SB_UNDERFLOW)

    def body(state):
        t, carries, accs = state
        carries, accs = step([(i - 2 - t, False)], carries, accs)
        return t + 1, carries, accs

    _, _, accs = lax.while_loop(cond, body, (jnp.int32(0), carries, accs))
    o_ref[...] = jnp.where(low, accs[0], accs[1]).astype(o_ref.dtype)


def _stick_breaking(qb, kb, vb, batch, seq):
    n, w = qb.shape
    blk = min(SB_QBLK, seq)
    nq = seq // blk
    tri = jnp.asarray(np.tril(np.ones((blk, blk), np.float32), -1), BF16)
    return pl.pallas_call(
        _sb_kernel,
        grid=(batch, w // LANES, nq),
        in_specs=[pl.BlockSpec((blk, LANES), lambda b, p, i: (b * nq + i, p)),
                  pl.BlockSpec((seq, LANES), lambda b, p, i: (b, p)),
                  pl.BlockSpec((seq, LANES), lambda b, p, i: (b, p)),
                  _resident((blk, blk))],
        out_specs=pl.BlockSpec((blk, LANES), lambda b, p, i: (b * nq + i, p)),
        out_shape=jax.ShapeDtypeStruct((n, w), BF16),
        compiler_params=_cparams("parallel", "parallel", "arbitrary"),
        name="stick_breaking",
    )(qb, kb, vb, tri)


def _gelu_tanh(x):
    return 0.5 * x * (1.0 + jnp.tanh(np.sqrt(2.0 / np.pi) * (x + 0.044715 * (x * x * x))))


def _lru_kernel(x_ref, g_ref, cw_ref, cb_ref, wri_ref, bri_ref, lam_ref, o_ref, tail_ref, h_ref):
    j = pl.program_id(1)
    ts, width = x_ref.shape

    @pl.when(j == 0)
    def _():
        tail_ref[...] = jnp.zeros_like(tail_ref)
        h_ref[...] = jnp.zeros_like(h_ref)

    x = x_ref[...].astype(F32)
    tail = tail_ref[...]
    row8 = lax.broadcasted_iota(jnp.int32, (SUBLANES, width), 0)
    y = x * cw_ref[CONV_W - 1:CONV_W, :] + cb_ref[...]
    for d in range(1, CONV_W):
        xs = pltpu.roll(x, d, 0)
        top = jnp.where(row8 < d, pltpu.roll(tail, d, 0), xs[:SUBLANES])
        xs = jnp.concatenate([top, xs[SUBLANES:]], axis=0)
        y = y + xs * cw_ref[CONV_W - 1 - d:CONV_W - d, :]
    tail_ref[...] = x[ts - SUBLANES:]

    ri = _dot(y.astype(BF16), wri_ref[...]) + bri_ref[...]
    r = jax.nn.sigmoid(ri[:, :width])
    gate_i = jax.nn.sigmoid(ri[:, width:])
    lam = lam_ref[...]
    softplus_neg_lam = jnp.maximum(-lam, 0.0) + jnp.log(1.0 + jnp.exp(-jnp.abs(lam)))
    a = jnp.exp2(r * ((-LRU_C * np.log2(np.e)) * softplus_neg_lam))
    u = jnp.sqrt(1.0 - a * a) * (gate_i * y)

    grouped = (ts // SUBLANES, SUBLANES, width)
    a, u = a.reshape(grouped), u.reshape(grouped)
    in_group = lax.broadcasted_iota(jnp.int32, grouped, 1)
    d = 1
    while d < SUBLANES:
        keep = in_group >= d
        a_sh = jnp.where(keep, pltpu.roll(a, d, 1), 1.0)
        u_sh = jnp.where(keep, pltpu.roll(u, d, 1), 0.0)
        u = a * u_sh + u
        a = a * a_sh
        d *= 2
    a, u = a.reshape(ts, width), u.reshape(ts, width)
    state = h_ref[...]
    groups = []
    for g in range(ts // SUBLANES):
        rows = slice(g * SUBLANES, (g + 1) * SUBLANES)
        hg = u[rows] + a[rows] * state
        groups.append(hg)
        state = hg[SUBLANES - 1:SUBLANES, :]
    h = jnp.concatenate(groups, axis=0)
    h_ref[...] = state
    o_ref[...] = (h * _gelu_tanh(g_ref[...].astype(F32))).astype(o_ref.dtype)


def _block_diag(w):
    heads, blk, _ = w.shape
    eye = jnp.eye(heads, dtype=w.dtype)
    return jnp.einsum('hij,hg->higj', w, eye).reshape(heads * blk, heads * blk)


def _rglru(xc, gc, conv_w, conv_b, w_r, b_r, w_i, b_i, lam, batch, seq):
    n, width = xc.shape
    ts = min(LRU_CHUNK, seq)
    nc = seq // ts
    wri = jnp.concatenate([_block_diag(w_r), _block_diag(w_i)], axis=1).astype(BF16)
    bri = jnp.concatenate([b_r, b_i]).astype(F32).reshape(1, 2 * width)
    blk = lambda b, j: (b * nc + j, 0)
    return pl.pallas_call(
        _lru_kernel,
        grid=(batch, nc),
        in_specs=[pl.BlockSpec((ts, width), blk), pl.BlockSpec((ts, width), blk),
                  _resident((CONV_W, width)), _resident((1, width)),
                  _resident((width, 2 * width)), _resident((1, 2 * width)),
                  _resident((1, width))],
        out_specs=pl.BlockSpec((ts, width), blk),
        out_shape=jax.ShapeDtypeStruct((n, width), BF16),
        scratch_shapes=[pltpu.VMEM((SUBLANES, width), F32), pltpu.VMEM((1, width), F32)],
        compiler_params=_cparams("parallel", "arbitrary"),
        name="rglru",
    )(xc, gc, conv_w.astype(F32), conv_b.astype(F32).reshape(1, width), wri, bri,
      lam.astype(F32).reshape(1, width))


def _merge_kernel(x_ref, oa_ref, ob_ref, oc_ref, gt_ref, bg_ref, wa_ref, wb_ref, wc_ref, wo_ref, o_ref):
    tm, d = x_ref.shape
    branches = ((oa_ref, wa_ref), (ob_ref, wb_ref), (oc_ref, wc_ref))
    halves = [slice(r * (tm // 2), (r + 1) * (tm // 2)) for r in range(2)]
    proj = [[_dot(b_ref[rows, :], w_ref[...]) for b_ref, w_ref in branches] for rows in halves]
    for rows, terms in zip(halves, proj):
        merged = None
        for idx, term in enumerate(terms):
            cols = slice(idx * d, (idx + 1) * d)
            gate = jax.nn.sigmoid(gt_ref[rows, cols].astype(F32) + bg_ref[:, cols])
            merged = gate * term if merged is None else merged + gate * term
        o_ref[rows, :] = x_ref[rows, :] + _dot(merged.astype(BF16), wo_ref[...])


def _merge(x2, o_a, o_b, o_c, gates, b_gate, wa, wb, wc, wo):
    n, d = x2.shape
    tm = min(ROW_TILE, n)
    row = lambda i: (i, 0)
    return pl.pallas_call(
        _merge_kernel,
        grid=(n // tm,),
        in_specs=[pl.BlockSpec((tm, d), row),
                  pl.BlockSpec((tm, o_a.shape[1]), row), pl.BlockSpec((tm, o_b.shape[1]), row),
                  pl.BlockSpec((tm, o_c.shape[1]), row), pl.BlockSpec((tm, gates.shape[1]), row),
                  _resident((1, gates.shape[1])),
                  _resident(wa.shape), _resident(wb.shape), _resident(wc.shape), _resident(wo.shape)],
        out_specs=pl.BlockSpec((tm, d), row),
        out_shape=jax.ShapeDtypeStruct((n, d), F32),
        compiler_params=_cparams("parallel"),
        name="merge",
    )(x2, o_a, o_b, o_c, gates, b_gate.astype(F32).reshape(1, -1), wa, wb, wc, wo)


def _swiglu_tile(h, wg_ref, wu_ref, wd_ref, ff_chunk):
    d_ff = wg_ref.shape[-1]
    starts = list(range(0, d_ff, ff_chunk))

    def gate_up(c):
        return _dot(h, wg_ref[:, c:c + ff_chunk]), _dot(h, wu_ref[:, c:c + ff_chunk])

    acc = None
    pending = gate_up(starts[0])
    for i, c in enumerate(starts):
        g, u = pending
        if i + 1 < len(starts):
            pending = gate_up(starts[i + 1])
        act = (g * jax.nn.sigmoid(g) * u).astype(BF16)
        part = _dot(act, wd_ref[c:c + ff_chunk, :])
        acc = part if acc is None else acc + part
    return acc


def _ffn_chunk(d_ff):
    for c in (512, 896, 256, 128):
        if d_ff % c == 0:
            return c
    return d_ff


def _dense_ffn_kernel(ff_chunk, x_ref, g_ref, wg_ref, wu_ref, wd_ref, o_ref):
    x = x_ref[...]
    h = _rms(x, g_ref[...]).astype(BF16)
    o_ref[...] = x + _swiglu_tile(h, wg_ref, wu_ref, wd_ref, ff_chunk)


def _dense_ffn(x2, gain, wg, wu, wd):
    n, d = x2.shape
    tm = min(ROW_TILE, n)
    return pl.pallas_call(
        functools.partial(_dense_ffn_kernel, _ffn_chunk(wg.shape[1])),
        grid=(n // tm,),
        in_specs=[pl.BlockSpec((tm, d), lambda i: (i, 0)), _resident((1, d)),
                  _resident(wg.shape), _resident(wu.shape), _resident(wd.shape)],
        out_specs=pl.BlockSpec((tm, d), lambda i: (i, 0)),
        out_shape=jax.ShapeDtypeStruct((n, d), F32),
        compiler_params=_cparams("parallel"),
        name="dense_ffn",
    )(x2, gain, wg, wu, wd)


def _router_kernel(n_experts, x_ref, g_ref, wr_ref, tril_ref, h_ref, idx_ref, wt_ref, total_ref, count_ref):
    h = _rms(x_ref[...], g_ref[...])
    h_ref[...] = h.astype(h_ref.dtype)
    w = wr_ref[...]
    h1 = h.astype(BF16)
    h2 = (h - h1.astype(F32)).astype(BF16)
    h3 = (h - h1.astype(F32) - h2.astype(F32)).astype(BF16)
    w1 = w.astype(BF16)
    w2 = (w - w1.astype(F32)).astype(BF16)
    w3 = (w - w1.astype(F32) - w2.astype(F32)).astype(BF16)
    logits = (_dot(h1, w1) + (_dot(h1, w2) + _dot(h2, w1))
              + (_dot(h2, w2) + _dot(h1, w3) + _dot(h3, w1)))
    lane = lax.broadcasted_iota(jnp.int32, logits.shape, 1)
    logits = jnp.where(lane < n_experts, logits, NEG_BIG)
    m1 = jnp.max(logits, axis=-1, keepdims=True)
    i1 = jnp.min(jnp.where(logits == m1, lane, LANES), axis=-1, keepdims=True)
    rest = jnp.where(lane == i1, NEG_BIG, logits)
    m2 = jnp.max(rest, axis=-1, keepdims=True)
    i2 = jnp.min(jnp.where(rest == m2, lane, LANES), axis=-1, keepdims=True)
    e2 = jnp.exp(m2 - m1)
    wt1 = 1.0 / (1.0 + e2)
    wt2 = e2 / (1.0 + e2)
    wt_ref[...] = jnp.where(lane == 0, wt1, jnp.where(lane == 1, wt2, 0.0))

    @pl.when(pl.program_id(0) == 0)
    def _():
        count_ref[...] = jnp.zeros_like(count_ref)

    chosen = jnp.where((lane == i1) | (lane == i2), 1.0, 0.0)
    before = _dot(tril_ref[...], chosen.astype(BF16)) + count_ref[...]
    r1 = jnp.sum(jnp.where(lane == i1, before, 0.0), axis=-1, keepdims=True).astype(jnp.int32)
    r2 = jnp.sum(jnp.where(lane == i2, before, 0.0), axis=-1, keepdims=True).astype(jnp.int32)
    count_ref[...] = count_ref[...] + jnp.sum(chosen, axis=0, keepdims=True)
    total_ref[...] = count_ref[...]
    route = jnp.where(lane == 0, i1, jnp.where(lane == 1, i2,
                      jnp.where(lane == 2, r1, jnp.where(lane == 3, r2, 0))))
    idx_ref[...] = jnp.transpose(route)[:idx_ref.shape[0], :]


def _router(x2, gain, w_router):
    n, d = x2.shape
    n_experts = w_router.shape[1]
    tm = min(ROW_TILE, n)
    wr = jnp.zeros((d, LANES), F32).at[:, :n_experts].set(w_router.astype(F32))
    tril = jnp.asarray(np.tril(np.ones((tm, tm), np.float32), -1), BF16)
    row = lambda i: (i, 0)
    return pl.pallas_call(
        functools.partial(_router_kernel, n_experts),
        grid=(n // tm,),
        in_specs=[pl.BlockSpec((tm, d), row), _resident((1, d)), _resident((d, LANES)), _resident((tm, tm))],
        out_specs=[pl.BlockSpec((tm, d), row), pl.BlockSpec((SUBLANES, tm), lambda i: (0, i)),
                   pl.BlockSpec((tm, LANES), row), pl.BlockSpec((1, LANES), lambda i: (0, 0))],
        out_shape=[jax.ShapeDtypeStruct((n, d), BF16), jax.ShapeDtypeStruct((SUBLANES, n), jnp.int32),
                   jax.ShapeDtypeStruct((n, LANES), F32), jax.ShapeDtypeStruct((1, LANES), F32)],
        scratch_shapes=[pltpu.VMEM((1, LANES), F32)],
        compiler_params=_cparams("arbitrary"),
        name="router",
    )(x2, gain, wr, tril)


def _moe_kernel(ff_chunk, te_ref, tv_ref, x_ref, wg_ref, wu_ref, wd_ref, o_ref):
    t = pl.program_id(0)

    @pl.when(tv_ref[t] > 0)
    def _():
        y = _swiglu_tile(x_ref[...], wg_ref.at[0], wu_ref.at[0], wd_ref.at[0], ff_chunk)
        o_ref[...] = y.astype(o_ref.dtype)

    @pl.when(tv_ref[t] == 0)
    def _():
        o_ref[...] = jnp.zeros_like(o_ref)


def _moe_experts(xs, tile_expert, tile_valid, wg, wu, wd):
    p, d = xs.shape
    d_ff = wg.shape[2]
    tm = MOE_TILE
    single = pl.Buffered(1)
    grid_spec = pltpu.PrefetchScalarGridSpec(
        num_scalar_prefetch=2,
        grid=(p // tm,),
        in_specs=[pl.BlockSpec((tm, d), lambda t, te, tv: (t, 0)),
                  pl.BlockSpec((1, d, d_ff), lambda t, te, tv: (te[t], 0, 0), pipeline_mode=single),
                  pl.BlockSpec((1, d, d_ff), lambda t, te, tv: (te[t], 0, 0), pipeline_mode=single),
                  pl.BlockSpec((1, d_ff, d), lambda t, te, tv: (te[t], 0, 0), pipeline_mode=single)],
        out_specs=pl.BlockSpec((tm, d), lambda t, te, tv: (t, 0)),
    )
    return pl.pallas_call(
        functools.partial(_moe_kernel, _ffn_chunk(d_ff)),
        grid_spec=grid_spec,
        out_shape=jax.ShapeDtypeStruct((p, d), BF16),
        compiler_params=_cparams("arbitrary"),
        name="moe_experts",
    )(tile_expert, tile_valid, xs, wg, wu, wd)


def _combine_kernel(x_ref, wt_ref, y0_ref, y1_ref, o_ref):
    wt = wt_ref[...]
    o_ref[...] = x_ref[...] + (wt[:, 0:1] * y0_ref[...].astype(F32) + wt[:, 1:2] * y1_ref[...].astype(F32))


def _combine(x2, wt, y0, y1):
    n, d = x2.shape
    tm = min(ROW_TILE, n)
    row = lambda i: (i, 0)
    return pl.pallas_call(
        _combine_kernel,
        grid=(n // tm,),
        in_specs=[pl.BlockSpec((tm, d), row), pl.BlockSpec((tm, LANES), row),
                  pl.BlockSpec((tm, d), row), pl.BlockSpec((tm, d), row)],
        out_specs=pl.BlockSpec((tm, d), row),
        out_shape=jax.ShapeDtypeStruct((n, d), F32),
        compiler_params=_cparams("parallel"),
        name="moe_combine",
    )(x2, wt, y0, y1)


def _moe_ffn(x2, gain, w_router, wg, wu, wd):
    n, d = x2.shape
    n_experts = w_router.shape[1]
    tm = MOE_TILE
    h, route, top_w, totals = _router(x2, gain, w_router)

    sizes = totals[0, :n_experts].astype(jnp.int32)
    padded = ((sizes + tm - 1) // tm) * tm
    starts = jnp.cumsum(padded) - padded
    pos = []
    for k in range(TOP_K):
        expert, rank = route[k], route[TOP_K + k]
        base = jnp.zeros_like(rank)
        for e in range(n_experts):
            base = base + jnp.where(expert == e, starts[e], 0)
        pos.append(rank + base)
    n_tiles = (n * TOP_K) // tm + n_experts
    p = n_tiles * tm
    token = jnp.tile(jnp.arange(n, dtype=jnp.int32), TOP_K)
    row_src = jnp.zeros((p,), jnp.int32).at[jnp.concatenate(pos)].set(
        token, unique_indices=True, mode="promise_in_bounds")
    tile_start = jnp.arange(n_tiles, dtype=jnp.int32) * tm
    ends = starts + padded
    tile_expert = jnp.minimum(jnp.sum((tile_start[:, None] >= ends[None, :]).astype(jnp.int32), axis=1),
                              n_experts - 1).astype(jnp.int32)
    tile_valid = (tile_start < ends[-1]).astype(jnp.int32)

    xs = jnp.take(h, row_src, axis=0)
    ys = _moe_experts(xs, tile_expert, tile_valid, wg, wu, wd)
    y0 = jnp.take(ys, pos[0], axis=0)
    y1 = jnp.take(ys, pos[1], axis=0)
    return _combine(x2, top_w, y0, y1)


def kernel(x, attn_norm, w_in, b_gate, q_norm, k_norm, sinks, conv_w, conv_b, lru_w_r, lru_b_r, lru_w_i, lru_b_i, lru_lambda, w_proj_a, w_proj_b, w_proj_c, w_out, ffn_norm, w_ffn_gate, w_ffn_up, w_ffn_down, w_router, w_exp_gate, w_exp_up, w_exp_down):
    batch, seq, d = x.shape
    depth = w_in.shape[0]
    swa_q = w_proj_a.shape[1]
    sb_w = w_proj_b.shape[1]
    lru_w = w_proj_c.shape[1]
    swa_kv = SWA_KV_HEADS * HEAD_DIM
    widths = (swa_q, swa_kv, swa_kv, sb_w, sb_w, sb_w, lru_w, lru_w, N_BRANCHES * d)
    assert sum(widths) == w_in.shape[2]
    assert seq % SB_QBLK == 0 and seq % WINDOW == 0

    x2 = x.reshape(batch * seq, d).astype(F32)
    for l in range(depth):
        gain = attn_norm[l].astype(F32).reshape(1, d)
        qa, ka, va, qb, kb, vb, xc, gc, gates = _in_proj(x2, gain, _to_bf16(w_in, l), widths,
                                                         q_norm[l], k_norm[l])
        o_a = _swa(qa, ka, va, sinks[l], batch, seq)
        o_b = _stick_breaking(qb, kb, vb, batch, seq)
        o_c = _rglru(xc, gc, conv_w[l], conv_b[l], lru_w_r[l], lru_b_r[l], lru_w_i[l], lru_b_i[l],
                     lru_lambda[l], batch, seq)
        x2 = _merge(x2, o_a, o_b, o_c, gates, b_gate[l],
                    _to_bf16(w_proj_a, l), _to_bf16(w_proj_b, l), _to_bf16(w_proj_c, l),
                    _to_bf16(w_out, l))
        fgain = ffn_norm[l].astype(F32).reshape(1, d)
        j = l // 2
        if l % 2 == 0:
            x2 = _dense_ffn(x2, fgain, _to_bf16(w_ffn_gate, j), _to_bf16(w_ffn_up, j),
                            _to_bf16(w_ffn_down, j))
        else:
            x2 = _moe_ffn(x2, fgain, w_router[j], w_exp_gate[j].astype(BF16), w_exp_up[j].astype(BF16),
                          w_exp_down[j].astype(BF16))
    return x2.reshape(batch, seq, d).astype(x.dtype)
```

```python
import functools

import numpy as np
import jax
import jax.numpy as jnp
from jax import lax
from jax.experimental import pallas as pl
from jax.experimental.pallas import tpu as pltpu

F32 = jnp.float32
BF16 = jnp.bfloat16

HEAD_DIM = 64
SWA_KV_HEADS = 2
WINDOW = 128
LRU_HEADS = 8
LRU_C = 8.0
CONV_W = 4
N_BRANCHES = 3
TOP_K = 2
EPS = 1e-6

LANES = 128
SUBLANES = 8
VMEM_LIMIT = 56 * 1024 * 1024
NEG_BIG = -1e30

SWA_BLOCKS_PER_STEP = 4
SB_QBLK = 256
SB_PAIRS_PER_STEP = 2
SB_UNDERFLOW = 105.0
LRU_CHUNK = 512
ROW_TILE = 512
MOE_TILE = 512


def _cparams(*sem):
    return pltpu.CompilerParams(dimension_semantics=sem, vmem_limit_bytes=VMEM_LIMIT)


def _resident(shape):
    nd = len(shape)
    return pl.BlockSpec(shape, lambda *_: (0,) * nd, pipeline_mode=pl.Buffered(1))


def _dot(a, b):
    return jnp.dot(a, b, preferred_element_type=F32)


def _dot_nt(a, b):
    return lax.dot_general(a, b, (((1,), (1,)), ((), ())), preferred_element_type=F32)


def _split_dot(a, b_bf16):
    hi = a.astype(BF16)
    lo = (a - hi.astype(F32)).astype(BF16)
    return _dot(hi, b_bf16) + _dot(lo, b_bf16)


def _rms(x, gain):
    ms = jnp.mean(x * x, axis=-1, keepdims=True)
    return x * lax.rsqrt(ms + EPS) * gain


CAST_BLOCK_BYTES = 4 * 1024 * 1024


def _cast_kernel(x_ref, o_ref):
    o_ref[...] = x_ref[...].astype(o_ref.dtype)


def _to_bf16(w, layer):
    shape = w.shape[1:]
    cols = shape[-1]
    rows = int(np.prod(shape[:-1]))
    tr = rows
    while tr * cols * 4 > CAST_BLOCK_BYTES and tr % 32 == 0:
        tr //= 2
    out = pl.pallas_call(
        _cast_kernel,
        grid=(rows // tr,),
        in_specs=[pl.BlockSpec((None, tr, cols), lambda i: (layer, i, 0))],
        out_specs=pl.BlockSpec((tr, cols), lambda i: (i, 0)),
        out_shape=jax.ShapeDtypeStruct((rows, cols), BF16),
        compiler_params=_cparams("parallel"),
        name="cast_bf16",
    )(w.reshape(w.shape[0], rows, cols))
    return out.reshape(shape)


def _head_norm(xf, bd, gain):
    ms = _dot((xf * xf).astype(BF16), bd)
    return xf * lax.rsqrt(ms + EPS) * gain


def _in_proj_kernel(widths, x_ref, g_ref, w_ref, bdq_ref, bdk_ref, qg_ref, kg_ref, *out_refs):
    h = _rms(x_ref[...], g_ref[...]).astype(BF16)
    normed = ((bdq_ref, qg_ref), (bdk_ref, kg_ref))
    offs = [sum(widths[:i]) for i in range(len(widths))]
    accs = [_dot(h, w_ref[:, offs[i]:offs[i] + widths[i]]) for i in range(len(normed))]
    for idx in range(len(normed), len(widths)):
        ref, width, off = out_refs[idx], widths[idx], offs[idx]
        for c in range(0, width, 1024):
            cw = min(1024, width - c)
            ref[:, c:c + cw] = _dot(h, w_ref[:, off + c:off + c + cw]).astype(ref.dtype)
    for idx, (bd_ref, gain_ref) in enumerate(normed):
        out_refs[idx][...] = _head_norm(accs[idx], bd_ref[...], gain_ref[...]).astype(out_refs[idx].dtype)


def _block_diag_mean(width):
    idx = np.arange(width) // HEAD_DIM
    return jnp.asarray((idx[:, None] == idx[None, :]).astype(np.float32) / HEAD_DIM, BF16)


def _in_proj(x2, gain, w_bf16, widths, q_norm, k_norm):
    n, d = x2.shape
    tm = min(ROW_TILE, n)
    qw, kw = widths[0], widths[1]
    qg = (jnp.tile(q_norm.astype(F32), qw // HEAD_DIM) * (HEAD_DIM ** -0.5)).reshape(1, qw)
    kg = jnp.tile(k_norm.astype(F32), kw // HEAD_DIM).reshape(1, kw)
    return pl.pallas_call(
        functools.partial(_in_proj_kernel, widths),
        grid=(n // tm,),
        in_specs=[pl.BlockSpec((tm, d), lambda i: (i, 0)),
                  _resident((1, d)),
                  _resident(w_bf16.shape),
                  _resident((qw, qw)), _resident((kw, kw)),
                  _resident((1, qw)), _resident((1, kw))],
        out_specs=[pl.BlockSpec((tm, w), lambda i: (i, 0)) for w in widths],
        out_shape=[jax.ShapeDtypeStruct((n, w), BF16) for w in widths],
        compiler_params=_cparams("parallel"),
        name="in_proj",
    )(x2, gain, w_bf16, _block_diag_mean(qw), _block_diag_mean(kw), qg, kg)


def _swa_kernel(n_heads, blk, q_ref, kc_ref, kp_ref, vc_ref, vp_ref, sink_ref, o_ref):
    n = pl.program_id(1)
    n_blocks = q_ref.shape[0] // blk
    group = n_heads // SWA_KV_HEADS

    upper = (lax.broadcasted_iota(jnp.int32, (blk, blk), 1)
             > lax.broadcasted_iota(jnp.int32, (blk, blk), 0))
    no_prev = upper & (n == 0)
    lane = lax.broadcasted_iota(jnp.int32, (blk, LANES), 1)
    low = lane < HEAD_DIM

    def swapped(t):
        return (t, pltpu.roll(t, HEAD_DIM, 1))

    rows = [slice(b * blk, (b + 1) * blk) for b in range(n_blocks)]
    k_blocks = [swapped(kp_ref[...])] + [swapped(kc_ref[r, :]) for r in rows]
    v_blocks = [swapped(vp_ref[...])] + [swapped(vc_ref[r, :]) for r in rows]
    kt_blocks = [tuple(jnp.transpose(t) for t in pair) for pair in k_blocks]

    jobs = [(b, j) for b in range(n_blocks) for j in range(n_heads)]
    sel = [(j // group + j % 2) % 2 for j in range(n_heads)]
    s_prev, s_cur, p_prev, p_cur, denom, outs = {}, {}, {}, {}, {}, {}

    def scores(t):
        b, j = jobs[t]
        q_pair = q_ref[rows[b], (j // 2) * LANES:(j // 2 + 1) * LANES]
        qm = jnp.where(low if j % 2 == 0 else ~low, q_pair, jnp.zeros_like(q_pair))
        s_prev[t] = _dot(qm, kt_blocks[b][sel[j]])
        s_cur[t] = _dot(qm, kt_blocks[b + 1][sel[j]])

    def softmax(t):
        b, j = jobs[t]
        s = jnp.where(upper, s_prev[t], s_cur[t])
        if b == 0:
            s = jnp.where(no_prev, NEG_BIG, s)
        sink = sink_ref[j]
        m = jnp.maximum(jnp.max(s, axis=-1, keepdims=True), sink)
        p = jnp.exp(s - m)
        denom[t] = jnp.sum(p, axis=-1, keepdims=True) + jnp.exp(sink - m)
        p_prev[t] = jnp.where(upper, p, 0.0).astype(BF16)
        p_cur[t] = jnp.where(upper, 0.0, p).astype(BF16)

    def values(t):
        b, j = jobs[t]
        outs[t] = (_dot(p_prev[t], v_blocks[b][sel[j]]), _dot(p_cur[t], v_blocks[b + 1][sel[j]]))

    def finish(t):
        b, j = jobs[t]
        outs[t] = (outs[t][0] + outs[t][1]) / denom[t]
        if j % 2 == 1:
            o_ref[rows[b], (j // 2) * LANES:(j // 2 + 1) * LANES] = jnp.where(
                low, outs[t - 1], outs[t]).astype(o_ref.dtype)

    for stage in (scores, softmax, values, finish):
        for t in range(len(jobs)):
            stage(t)


def _swa(qa, ka, va, sinks, batch, seq):
    n, qw = qa.shape
    kw = ka.shape[1]
    n_heads = qw // HEAD_DIM
    blk = WINDOW
    per_step = min(SWA_BLOCKS_PER_STEP, seq // blk)
    assert seq % (per_step * blk) == 0
    rows = per_step * blk
    ns = seq // rows
    cur = lambda b, i: (b * ns + i, 0)
    prev = lambda b, i: (b * ns * per_step + jnp.maximum(i * per_step - 1, 0), 0)
    return pl.pallas_call(
        functools.partial(_swa_kernel, n_heads, blk),
        grid=(batch, ns),
        in_specs=[pl.BlockSpec((rows, qw), cur),
                  pl.BlockSpec((rows, kw), cur), pl.BlockSpec((blk, kw), prev),
                  pl.BlockSpec((rows, kw), cur), pl.BlockSpec((blk, kw), prev),
                  pl.BlockSpec(memory_space=pltpu.SMEM)],
        out_specs=pl.BlockSpec((rows, qw), cur),
        out_shape=jax.ShapeDtypeStruct((n, qw), BF16),
        compiler_params=_cparams("parallel", "parallel"),
        name="swa",
    )(qa, ka, ka, va, va, sinks.astype(F32))


def _sb_kernel(q_ref, k_ref, v_ref, tri_ref, o_ref):
    i = pl.program_id(2)
    qb, width = q_ref.shape
    n_heads = 2 * (width // LANES)
    lane = lax.broadcasted_iota(jnp.int32, (qb, LANES), 1)
    low = lane < HEAD_DIM
    scale = jnp.asarray(HEAD_DIM ** -0.5, q_ref.dtype)
    q_heads = []
    for h in range(n_heads):
        q_pair = q_ref[:, (h // 2) * LANES:(h // 2 + 1) * LANES]
        q_heads.append(jnp.where(low if h % 2 == 0 else ~low, q_pair, jnp.zeros_like(q_pair)) * scale)
    tri = tri_ref[...]
    strict = (lax.broadcasted_iota(jnp.int32, (qb, qb), 1)
              < lax.broadcasted_iota(jnp.int32, (qb, qb), 0))

    def step(chunks, carries, accs):
        carries, accs = list(carries), list(accs)
        jobs = [(ci, h) for ci in range(len(chunks)) for h in range(n_heads)]
        z, sp, spm, later, w = {}, {}, {}, {}, {}

        def cols(ref, ci, h):
            start = pl.multiple_of(chunks[ci][0] * qb, qb)
            return ref[pl.ds(start, qb), (h // 2) * LANES:(h // 2 + 1) * LANES]

        def scores(j):
            ci, h = jobs[j]
            z[j] = _dot_nt(q_heads[h], cols(k_ref, ci, h))

        def softplus(j):
            sp[j] = jnp.maximum(z[j], 0.0) + jnp.log(1.0 + jnp.exp(-jnp.abs(z[j])))
            spm[j] = jnp.where(strict, sp[j], 0.0) if chunks[jobs[j][0]][1] else sp[j]

        def suffix_sums(j):
            later[j] = _dot(spm[j].astype(BF16), tri)

        def weights(j):
            ci, h = jobs[j]
            wj = jnp.exp(z[j] - sp[j] - (later[j] + carries[h]))
            w[j] = (jnp.where(strict, wj, 0.0) if chunks[ci][1] else wj).astype(BF16)
            carries[h] = carries[h] + jnp.sum(spm[j], axis=-1, keepdims=True)

        def values(j):
            ci, h = jobs[j]
            accs[h] = accs[h] + _dot(w[j], cols(v_ref, ci, h))

        for stage in (scores, softplus, suffix_sums, weights, values):
            for j in range(len(jobs)):
                stage(j)
        return tuple(carries), tuple(accs)

    carries = (jnp.zeros((qb, 1), F32),) * n_heads
    accs = (jnp.zeros((qb, LANES), F32),) * n_heads
    carries, accs = lax.cond(i == 0,
                             lambda: step([(i, True)], carries, accs),
                             lambda: step([(i, True), (i - 1, False)], carries, accs))

    def cond(state):
        t, carries, _ = state
        lowest = functools.reduce(jnp.minimum, [jnp.min(c) for c in carries])
        return (t < i - 1) & (lowest <= SB_UNDERFLOW)

    def body(state):
        t, carries, accs = state
        carries, accs = step([(i - 2 - t, False)], carries, accs)
        return t + 1, carries, accs

    _, _, accs = lax.while_loop(cond, body, (jnp.int32(0), carries, accs))
    for p in range(width // LANES):
        o_ref[:, p * LANES:(p + 1) * LANES] = jnp.where(low, accs[2 * p], accs[2 * p + 1]).astype(o_ref.dtype)


def _stick_breaking(qb, kb, vb, batch, seq):
    n, w = qb.shape
    blk = min(SB_QBLK, seq)
    nq = seq // blk
    cols = SB_PAIRS_PER_STEP * LANES
    tri = jnp.asarray(np.tril(np.ones((blk, blk), np.float32), -1), BF16)
    return pl.pallas_call(
        _sb_kernel,
        grid=(batch, w // cols, nq),
        in_specs=[pl.BlockSpec((blk, cols), lambda b, p, i: (b * nq + i, p)),
                  pl.BlockSpec((seq, cols), lambda b, p, i: (b, p)),
                  pl.BlockSpec((seq, cols), lambda b, p, i: (b, p)),
                  _resident((blk, blk))],
        out_specs=pl.BlockSpec((blk, cols), lambda b, p, i: (b * nq + i, p)),
        out_shape=jax.ShapeDtypeStruct((n, w), BF16),
        compiler_params=_cparams("parallel", "parallel", "arbitrary"),
        name="stick_breaking",
    )(qb, kb, vb, tri)


def _gelu_tanh(x):
    return 0.5 * x * (1.0 + jnp.tanh(np.sqrt(2.0 / np.pi) * (x + 0.044715 * (x * x * x))))


def _lru_kernel(x_ref, g_ref, cw_ref, cb_ref, wri_ref, bri_ref, lam_ref, o_ref, tail_ref, h_ref):
    j = pl.program_id(1)
    ts, width = x_ref.shape

    @pl.when(j == 0)
    def _():
        tail_ref[...] = jnp.zeros_like(tail_ref)
        h_ref[...] = jnp.zeros_like(h_ref)

    x = x_ref[...].astype(F32)
    tail = tail_ref[...]
    row8 = lax.broadcasted_iota(jnp.int32, (SUBLANES, width), 0)
    y = x * cw_ref[CONV_W - 1:CONV_W, :] + cb_ref[...]
    for d in range(1, CONV_W):
        xs = pltpu.roll(x, d, 0)
        top = jnp.where(row8 < d, pltpu.roll(tail, d, 0), xs[:SUBLANES])
        xs = jnp.concatenate([top, xs[SUBLANES:]], axis=0)
        y = y + xs * cw_ref[CONV_W - 1 - d:CONV_W - d, :]
    tail_ref[...] = x[ts - SUBLANES:]

    ri = _dot(y.astype(BF16), wri_ref[...]) + bri_ref[...]
    r = jax.nn.sigmoid(ri[:, :width])
    gate_i = jax.nn.sigmoid(ri[:, width:])
    lam = lam_ref[...]
    softplus_neg_lam = jnp.maximum(-lam, 0.0) + jnp.log(1.0 + jnp.exp(-jnp.abs(lam)))
    a = jnp.exp2(r * ((-LRU_C * np.log2(np.e)) * softplus_neg_lam))
    u = jnp.sqrt(1.0 - a * a) * (gate_i * y)

    grouped = (ts // SUBLANES, SUBLANES, width)
    a, u = a.reshape(grouped), u.reshape(grouped)
    in_group = lax.broadcasted_iota(jnp.int32, grouped, 1)
    d = 1
    while d < SUBLANES:
        keep = in_group >= d
        a_sh = jnp.where(keep, pltpu.roll(a, d, 1), 1.0)
        u_sh = jnp.where(keep, pltpu.roll(u, d, 1), 0.0)
        u = a * u_sh + u
        a = a * a_sh
        d *= 2
    a, u = a.reshape(ts, width), u.reshape(ts, width)
    state = h_ref[...]
    groups = []
    for g in range(ts // SUBLANES):
        rows = slice(g * SUBLANES, (g + 1) * SUBLANES)
        hg = u[rows] + a[rows] * state
        groups.append(hg)
        state = hg[SUBLANES - 1:SUBLANES, :]
    h = jnp.concatenate(groups, axis=0)
    h_ref[...] = state
    o_ref[...] = (h * _gelu_tanh(g_ref[...].astype(F32))).astype(o_ref.dtype)


def _block_diag(w):
    heads, blk, _ = w.shape
    eye = jnp.eye(heads, dtype=w.dtype)
    return jnp.einsum('hij,hg->higj', w, eye).reshape(heads * blk, heads * blk)


def _rglru(xc, gc, conv_w, conv_b, w_r, b_r, w_i, b_i, lam, batch, seq):
    n, width = xc.shape
    ts = min(LRU_CHUNK, seq)
    nc = seq // ts
    wri = jnp.concatenate([_block_diag(w_r), _block_diag(w_i)], axis=1).astype(BF16)
    bri = jnp.concatenate([b_r, b_i]).astype(F32).reshape(1, 2 * width)
    blk = lambda b, j: (b * nc + j, 0)
    return pl.pallas_call(
        _lru_kernel,
        grid=(batch, nc),
        in_specs=[pl.BlockSpec((ts, width), blk), pl.BlockSpec((ts, width), blk),
                  _resident((CONV_W, width)), _resident((1, width)),
                  _resident((width, 2 * width)), _resident((1, 2 * width)),
                  _resident((1, width))],
        out_specs=pl.BlockSpec((ts, width), blk),
        out_shape=jax.ShapeDtypeStruct((n, width), BF16),
        scratch_shapes=[pltpu.VMEM((SUBLANES, width), F32), pltpu.VMEM((1, width), F32)],
        compiler_params=_cparams("parallel", "arbitrary"),
        name="rglru",
    )(xc, gc, conv_w.astype(F32), conv_b.astype(F32).reshape(1, width), wri, bri,
      lam.astype(F32).reshape(1, width))


def _merge_kernel(x_ref, oa_ref, ob_ref, oc_ref, gt_ref, bg_ref, wa_ref, wb_ref, wc_ref, wo_ref, o_ref):
    tm, d = x_ref.shape
    branches = ((oa_ref, wa_ref), (ob_ref, wb_ref), (oc_ref, wc_ref))
    halves = [slice(r * (tm // 2), (r + 1) * (tm // 2)) for r in range(2)]
    proj = [[_dot(b_ref[rows, :], w_ref[...]) for b_ref, w_ref in branches] for rows in halves]
    for rows, terms in zip(halves, proj):
        merged = None
        for idx, term in enumerate(terms):
            cols = slice(idx * d, (idx + 1) * d)
            gate = jax.nn.sigmoid(gt_ref[rows, cols].astype(F32) + bg_ref[:, cols])
            merged = gate * term if merged is None else merged + gate * term
        o_ref[rows, :] = x_ref[rows, :] + _dot(merged.astype(BF16), wo_ref[...])


def _merge(x2, o_a, o_b, o_c, gates, b_gate, wa, wb, wc, wo):
    n, d = x2.shape
    tm = min(ROW_TILE, n)
    row = lambda i: (i, 0)
    return pl.pallas_call(
        _merge_kernel,
        grid=(n // tm,),
        in_specs=[pl.BlockSpec((tm, d), row),
                  pl.BlockSpec((tm, o_a.shape[1]), row), pl.BlockSpec((tm, o_b.shape[1]), row),
                  pl.BlockSpec((tm, o_c.shape[1]), row), pl.BlockSpec((tm, gates.shape[1]), row),
                  _resident((1, gates.shape[1])),
                  _resident(wa.shape), _resident(wb.shape), _resident(wc.shape), _resident(wo.shape)],
        out_specs=pl.BlockSpec((tm, d), row),
        out_shape=jax.ShapeDtypeStruct((n, d), F32),
        compiler_params=_cparams("parallel"),
        name="merge",
    )(x2, o_a, o_b, o_c, gates, b_gate.astype(F32).reshape(1, -1), wa, wb, wc, wo)


def _swiglu_tile(h, wg_ref, wu_ref, wd_ref, ff_chunk):
    d_ff = wg_ref.shape[-1]
    starts = list(range(0, d_ff, ff_chunk))

    def gate_up(c):
        return _dot(h, wg_ref[:, c:c + ff_chunk]), _dot(h, wu_ref[:, c:c + ff_chunk])

    acc = None
    pending = gate_up(starts[0])
    for i, c in enumerate(starts):
        g, u = pending
        if i + 1 < len(starts):
            pending = gate_up(starts[i + 1])
        act = (g * jax.nn.sigmoid(g) * u).astype(BF16)
        part = _dot(act, wd_ref[c:c + ff_chunk, :])
        acc = part if acc is None else acc + part
    return acc


def _ffn_chunk(d_ff):
    for c in (512, 896, 256, 128):
        if d_ff % c == 0:
            return c
    return d_ff


def _dense_ffn_kernel(ff_chunk, x_ref, g_ref, wg_ref, wu_ref, wd_ref, o_ref):
    x = x_ref[...]
    h = _rms(x, g_ref[...]).astype(BF16)
    o_ref[...] = x + _swiglu_tile(h, wg_ref, wu_ref, wd_ref, ff_chunk)


def _dense_ffn(x2, gain, wg, wu, wd):
    n, d = x2.shape
    tm = min(ROW_TILE, n)
    return pl.pallas_call(
        functools.partial(_dense_ffn_kernel, _ffn_chunk(wg.shape[1])),
        grid=(n // tm,),
        in_specs=[pl.BlockSpec((tm, d), lambda i: (i, 0)), _resident((1, d)),
                  _resident(wg.shape), _resident(wu.shape), _resident(wd.shape)],
        out_specs=pl.BlockSpec((tm, d), lambda i: (i, 0)),
        out_shape=jax.ShapeDtypeStruct((n, d), F32),
        compiler_params=_cparams("parallel"),
        name="dense_ffn",
    )(x2, gain, wg, wu, wd)


def _router_kernel(n_experts, stride, x_ref, g_ref, wr_ref, tril_ref, h_ref, idx_ref, wt_ref, total_ref, count_ref):
    h = _rms(x_ref[...], g_ref[...])
    h_ref[...] = h.astype(h_ref.dtype)
    w = wr_ref[...]
    h1 = h.astype(BF16)
    h2 = (h - h1.astype(F32)).astype(BF16)
    h3 = (h - h1.astype(F32) - h2.astype(F32)).astype(BF16)
    a, b, c = _dot(h1, w), _dot(h2, w), _dot(h3, w)

    def term(x, k):
        return x if k == 0 else pltpu.roll(x, LANES - k * stride, 1)

    logits = ((term(a, 0) + (term(a, 1) + term(b, 0)))
              + (term(b, 1) + term(a, 2) + term(c, 0)))
    lane = lax.broadcasted_iota(jnp.int32, logits.shape, 1)
    logits = jnp.where(lane < n_experts, logits, NEG_BIG)
    m1 = jnp.max(logits, axis=-1, keepdims=True)
    i1 = jnp.min(jnp.where(logits == m1, lane, LANES), axis=-1, keepdims=True)
    rest = jnp.where(lane == i1, NEG_BIG, logits)
    m2 = jnp.max(rest, axis=-1, keepdims=True)
    i2 = jnp.min(jnp.where(rest == m2, lane, LANES), axis=-1, keepdims=True)
    e2 = jnp.exp(m2 - m1)
    wt1 = 1.0 / (1.0 + e2)
    wt2 = e2 / (1.0 + e2)
    wt_ref[...] = jnp.where(lane == 0, wt1, jnp.where(lane == 1, wt2, 0.0))

    @pl.when(pl.program_id(0) == 0)
    def _():
        count_ref[...] = jnp.zeros_like(count_ref)

    chosen = jnp.where((lane == i1) | (lane == i2), 1.0, 0.0)
    before = _dot(tril_ref[...], chosen.astype(BF16)) + count_ref[...]
    r1 = jnp.sum(jnp.where(lane == i1, before, 0.0), axis=-1, keepdims=True).astype(jnp.int32)
    r2 = jnp.sum(jnp.where(lane == i2, before, 0.0), axis=-1, keepdims=True).astype(jnp.int32)
    count_ref[...] = count_ref[...] + jnp.sum(chosen, axis=0, keepdims=True)
    total_ref[...] = count_ref[...]
    route = jnp.where(lane == 0, i1, jnp.where(lane == 1, i2,
                      jnp.where(lane == 2, r1, jnp.where(lane == 3, r2, 0))))
    idx_ref[...] = jnp.transpose(route)[:idx_ref.shape[0], :]


def _router(x2, gain, w_router):
    n, d = x2.shape
    n_experts = w_router.shape[1]
    tm = min(ROW_TILE, n)
    stride = SUBLANES * (-(-n_experts // SUBLANES))
    assert 3 * stride <= LANES
    w = w_router.astype(F32)
    w1 = w.astype(BF16)
    w2 = (w - w1.astype(F32)).astype(BF16)
    w3 = (w - w1.astype(F32) - w2.astype(F32)).astype(BF16)
    wr = jnp.zeros((d, LANES), BF16)
    for k, wk in enumerate((w1, w2, w3)):
        wr = wr.at[:, k * stride:k * stride + n_experts].set(wk)
    tril = jnp.asarray(np.tril(np.ones((tm, tm), np.float32), -1), BF16)
    row = lambda i: (i, 0)
    return pl.pallas_call(
        functools.partial(_router_kernel, n_experts, stride),
        grid=(n // tm,),
        in_specs=[pl.BlockSpec((tm, d), row), _resident((1, d)), _resident((d, LANES)), _resident((tm, tm))],
        out_specs=[pl.BlockSpec((tm, d), row), pl.BlockSpec((SUBLANES, tm), lambda i: (0, i)),
                   pl.BlockSpec((tm, LANES), row), pl.BlockSpec((1, LANES), lambda i: (0, 0))],
        out_shape=[jax.ShapeDtypeStruct((n, d), BF16), jax.ShapeDtypeStruct((SUBLANES, n), jnp.int32),
                   jax.ShapeDtypeStruct((n, LANES), F32), jax.ShapeDtypeStruct((1, LANES), F32)],
        scratch_shapes=[pltpu.VMEM((1, LANES), F32)],
        compiler_params=_cparams("arbitrary"),
        name="router",
    )(x2, gain, wr, tril)


def _moe_kernel(ff_chunk, te_ref, tv_ref, x_ref, wg_ref, wu_ref, wd_ref, o_ref):
    t = pl.program_id(0)

    @pl.when(tv_ref[t] > 0)
    def _():
        y = _swiglu_tile(x_ref[...], wg_ref.at[0], wu_ref.at[0], wd_ref.at[0], ff_chunk)
        o_ref[...] = y.astype(o_ref.dtype)

    @pl.when(tv_ref[t] == 0)
    def _():
        o_ref[...] = jnp.zeros_like(o_ref)


def _moe_experts(xs, tile_expert, tile_valid, wg, wu, wd):
    p, d = xs.shape
    d_ff = wg.shape[2]
    tm = MOE_TILE
    single = pl.Buffered(1)
    grid_spec = pltpu.PrefetchScalarGridSpec(
        num_scalar_prefetch=2,
        grid=(p // tm,),
        in_specs=[pl.BlockSpec((tm, d), lambda t, te, tv: (t, 0)),
                  pl.BlockSpec((1, d, d_ff), lambda t, te, tv: (te[t], 0, 0), pipeline_mode=single),
                  pl.BlockSpec((1, d, d_ff), lambda t, te, tv: (te[t], 0, 0), pipeline_mode=single),
                  pl.BlockSpec((1, d_ff, d), lambda t, te, tv: (te[t], 0, 0), pipeline_mode=single)],
        out_specs=pl.BlockSpec((tm, d), lambda t, te, tv: (t, 0)),
    )
    return pl.pallas_call(
        functools.partial(_moe_kernel, _ffn_chunk(d_ff)),
        grid_spec=grid_spec,
        out_shape=jax.ShapeDtypeStruct((p, d), BF16),
        compiler_params=_cparams("arbitrary"),
        name="moe_experts",
    )(tile_expert, tile_valid, xs, wg, wu, wd)


def _combine_kernel(x_ref, wt_ref, y0_ref, y1_ref, o_ref):
    wt = wt_ref[...]
    o_ref[...] = x_ref[...] + (wt[:, 0:1] * y0_ref[...].astype(F32) + wt[:, 1:2] * y1_ref[...].astype(F32))


def _combine(x2, wt, y0, y1):
    n, d = x2.shape
    tm = min(ROW_TILE, n)
    row = lambda i: (i, 0)
    return pl.pallas_call(
        _combine_kernel,
        grid=(n // tm,),
        in_specs=[pl.BlockSpec((tm, d), row), pl.BlockSpec((tm, LANES), row),
                  pl.BlockSpec((tm, d), row), pl.BlockSpec((tm, d), row)],
        out_specs=pl.BlockSpec((tm, d), row),
        out_shape=jax.ShapeDtypeStruct((n, d), F32),
        compiler_params=_cparams("parallel"),
        name="moe_combine",
    )(x2, wt, y0, y1)


def _moe_ffn(x2, gain, w_router, wg, wu, wd):
    n, d = x2.shape
    n_experts = w_router.shape[1]
    tm = MOE_TILE
    h, route, top_w, totals = _router(x2, gain, w_router)

    sizes = totals[0, :n_experts].astype(jnp.int32)
    padded = ((sizes + tm - 1) // tm) * tm
    starts = jnp.cumsum(padded) - padded
    pos = []
    for k in range(TOP_K):
        expert, rank = route[k], route[TOP_K + k]
        base = jnp.zeros_like(rank)
        for e in range(n_experts):
            base = base + jnp.where(expert == e, starts[e], 0)
        pos.append(rank + base)
    n_tiles = (n * TOP_K) // tm + n_experts
    p = n_tiles * tm
    token = jnp.tile(jnp.arange(n, dtype=jnp.int32), TOP_K)
    row_src = jnp.zeros((p,), jnp.int32).at[jnp.concatenate(pos)].set(
        token, unique_indices=True, mode="promise_in_bounds")
    tile_start = jnp.arange(n_tiles, dtype=jnp.int32) * tm
    ends = starts + padded
    tile_expert = jnp.minimum(jnp.sum((tile_start[:, None] >= ends[None, :]).astype(jnp.int32), axis=1),
                              n_experts - 1).astype(jnp.int32)
    tile_valid = (tile_start < ends[-1]).astype(jnp.int32)

    xs = h.at[row_src].get(mode="promise_in_bounds")
    ys = _moe_experts(xs, tile_expert, tile_valid, wg, wu, wd)
    y0 = ys.at[pos[0]].get(mode="promise_in_bounds")
    y1 = ys.at[pos[1]].get(mode="promise_in_bounds")
    return _combine(x2, top_w, y0, y1)


def kernel(x, attn_norm, w_in, b_gate, q_norm, k_norm, sinks, conv_w, conv_b, lru_w_r, lru_b_r, lru_w_i, lru_b_i, lru_lambda, w_proj_a, w_proj_b, w_proj_c, w_out, ffn_norm, w_ffn_gate, w_ffn_up, w_ffn_down, w_router, w_exp_gate, w_exp_up, w_exp_down):
    batch, seq, d = x.shape
    depth = w_in.shape[0]
    swa_q = w_proj_a.shape[1]
    sb_w = w_proj_b.shape[1]
    lru_w = w_proj_c.shape[1]
    swa_kv = SWA_KV_HEADS * HEAD_DIM
    widths = (swa_q, swa_kv, swa_kv, sb_w, sb_w, sb_w, lru_w, lru_w, N_BRANCHES * d)
    assert sum(widths) == w_in.shape[2]
    assert seq % SB_QBLK == 0 and seq % WINDOW == 0

    x2 = x.reshape(batch * seq, d).astype(F32)
    for l in range(depth):
        gain = attn_norm[l].astype(F32).reshape(1, d)
        qa, ka, va, qb, kb, vb, xc, gc, gates = _in_proj(x2, gain, _to_bf16(w_in, l), widths,
                                                         q_norm[l], k_norm[l])
        o_a = _swa(qa, ka, va, sinks[l], batch, seq)
        o_b = _stick_breaking(qb, kb, vb, batch, seq)
        o_c = _rglru(xc, gc, conv_w[l], conv_b[l], lru_w_r[l], lru_b_r[l], lru_w_i[l], lru_b_i[l],
                     lru_lambda[l], batch, seq)
        x2 = _merge(x2, o_a, o_b, o_c, gates, b_gate[l],
                    _to_bf16(w_proj_a, l), _to_bf16(w_proj_b, l), _to_bf16(w_proj_c, l),
                    _to_bf16(w_out, l))
        fgain = ffn_norm[l].astype(F32).reshape(1, d)
        j = l // 2
        if l % 2 == 0:
            x2 = _dense_ffn(x2, fgain, _to_bf16(w_ffn_gate, j), _to_bf16(w_ffn_up, j),
                            _to_bf16(w_ffn_down, j))
        else:
            x2 = _moe_ffn(x2, fgain, w_router[j], w_exp_gate[j].astype(BF16), w_exp_up[j].astype(BF16),
                          w_exp_down[j].astype(BF16))
    return x2.reshape(batch, seq, d).astype(x.dtype)
```

```python
import functools

import numpy as np
import jax
import jax.numpy as jnp
from jax import lax
from jax.experimental import pallas as pl
from jax.experimental.pallas import tpu as pltpu

F32 = jnp.float32
BF16 = jnp.bfloat16

HEAD_DIM = 64
SWA_KV_HEADS = 2
WINDOW = 128
LRU_HEADS = 8
LRU_C = 8.0
CONV_W = 4
N_BRANCHES = 3
TOP_K = 2
EPS = 1e-6

LANES = 128
SUBLANES = 8
VMEM_LIMIT = 56 * 1024 * 1024
NEG_BIG = -1e30

SWA_BLOCKS_PER_STEP = 4
SB_QBLK = 256
SB_PAIRS_PER_STEP = 4
SB_UNDERFLOW = 105.0
LRU_CHUNK = 512
ROW_TILE = 512
MOE_TILE = 512


def _cparams(*sem):
    return pltpu.CompilerParams(dimension_semantics=sem, vmem_limit_bytes=VMEM_LIMIT)


def _resident(shape):
    nd = len(shape)
    return pl.BlockSpec(shape, lambda *_: (0,) * nd, pipeline_mode=pl.Buffered(1))


def _dot(a, b):
    return jnp.dot(a, b, preferred_element_type=F32)


def _dot_nt(a, b):
    return lax.dot_general(a, b, (((1,), (1,)), ((), ())), preferred_element_type=F32)


def _split_dot(a, b_bf16):
    hi = a.astype(BF16)
    lo = (a - hi.astype(F32)).astype(BF16)
    return _dot(hi, b_bf16) + _dot(lo, b_bf16)


def _rms(x, gain):
    ms = jnp.mean(x * x, axis=-1, keepdims=True)
    return x * lax.rsqrt(ms + EPS) * gain


CAST_BLOCK_BYTES = 4 * 1024 * 1024


def _cast_kernel(after_ref, x_ref, o_ref):
    del after_ref
    o_ref[...] = x_ref[...].astype(o_ref.dtype)


def _to_bf16(w, layer, after=None):
    shape = w.shape[1:]
    cols = shape[-1]
    rows = int(np.prod(shape[:-1]))
    tr = rows
    while tr * cols * 4 > CAST_BLOCK_BYTES and tr % 32 == 0:
        tr //= 2
    token = jnp.zeros((1,), jnp.int32) if after is None else after.reshape(-1)[:1]
    out = pl.pallas_call(
        _cast_kernel,
        grid=(rows // tr,),
        in_specs=[pl.BlockSpec(memory_space=pltpu.SMEM),
                  pl.BlockSpec((None, tr, cols), lambda i: (layer, i, 0))],
        out_specs=pl.BlockSpec((tr, cols), lambda i: (i, 0)),
        out_shape=jax.ShapeDtypeStruct((rows, cols), BF16),
        compiler_params=_cparams("parallel"),
        name="cast_bf16",
    )(token, w.reshape(w.shape[0], rows, cols))
    return out.reshape(shape)


def _head_norm(xf, bd, gain):
    ms = _dot((xf * xf).astype(BF16), bd)
    return xf * lax.rsqrt(ms + EPS) * gain


def _in_proj_kernel(widths, x_ref, g_ref, w_ref, bdq_ref, bdk_ref, qg_ref, kg_ref, *out_refs):
    h = _rms(x_ref[...], g_ref[...]).astype(BF16)
    normed = ((bdq_ref, qg_ref), (bdk_ref, kg_ref))
    offs = [sum(widths[:i]) for i in range(len(widths))]
    accs = [_dot(h, w_ref[:, offs[i]:offs[i] + widths[i]]) for i in range(len(normed))]
    for idx in range(len(normed), len(widths)):
        ref, width, off = out_refs[idx], widths[idx], offs[idx]
        for c in range(0, width, 1024):
            cw = min(1024, width - c)
            ref[:, c:c + cw] = _dot(h, w_ref[:, off + c:off + c + cw]).astype(ref.dtype)
    for idx, (bd_ref, gain_ref) in enumerate(normed):
        out_refs[idx][...] = _head_norm(accs[idx], bd_ref[...], gain_ref[...]).astype(out_refs[idx].dtype)


def _block_diag_mean(width):
    idx = np.arange(width) // HEAD_DIM
    return jnp.asarray((idx[:, None] == idx[None, :]).astype(np.float32) / HEAD_DIM, BF16)


def _in_proj(x2, gain, w_bf16, widths, q_norm, k_norm):
    n, d = x2.shape
    tm = min(ROW_TILE, n)
    qw, kw = widths[0], widths[1]
    qg = (jnp.tile(q_norm.astype(F32), qw // HEAD_DIM) * (HEAD_DIM ** -0.5)).reshape(1, qw)
    kg = jnp.tile(k_norm.astype(F32), kw // HEAD_DIM).reshape(1, kw)
    return pl.pallas_call(
        functools.partial(_in_proj_kernel, widths),
        grid=(n // tm,),
        in_specs=[pl.BlockSpec((tm, d), lambda i: (i, 0)),
                  _resident((1, d)),
                  _resident(w_bf16.shape),
                  _resident((qw, qw)), _resident((kw, kw)),
                  _resident((1, qw)), _resident((1, kw))],
        out_specs=[pl.BlockSpec((tm, w), lambda i: (i, 0)) for w in widths],
        out_shape=[jax.ShapeDtypeStruct((n, w), BF16) for w in widths],
        compiler_params=_cparams("parallel"),
        name="in_proj",
    )(x2, gain, w_bf16, _block_diag_mean(qw), _block_diag_mean(kw), qg, kg)


def _swa_kernel(n_heads, blk, q_ref, kc_ref, kp_ref, vc_ref, vp_ref, sink_ref, o_ref):
    n = pl.program_id(1)
    n_blocks = q_ref.shape[0] // blk
    group = n_heads // SWA_KV_HEADS

    upper = (lax.broadcasted_iota(jnp.int32, (blk, blk), 1)
             > lax.broadcasted_iota(jnp.int32, (blk, blk), 0))
    no_prev = upper & (n == 0)
    lane = lax.broadcasted_iota(jnp.int32, (blk, LANES), 1)
    low = lane < HEAD_DIM

    def swapped(t):
        return (t, pltpu.roll(t, HEAD_DIM, 1))

    rows = [slice(b * blk, (b + 1) * blk) for b in range(n_blocks)]
    k_blocks = [swapped(kp_ref[...])] + [swapped(kc_ref[r, :]) for r in rows]
    v_blocks = [swapped(vp_ref[...])] + [swapped(vc_ref[r, :]) for r in rows]
    kt_blocks = [tuple(jnp.transpose(t) for t in pair) for pair in k_blocks]

    jobs = [(b, j) for b in range(n_blocks) for j in range(n_heads)]
    sel = [(j // group + j % 2) % 2 for j in range(n_heads)]
    s_prev, s_cur, p_prev, p_cur, denom, outs = {}, {}, {}, {}, {}, {}

    def scores(t):
        b, j = jobs[t]
        q_pair = q_ref[rows[b], (j // 2) * LANES:(j // 2 + 1) * LANES]
        qm = jnp.where(low if j % 2 == 0 else ~low, q_pair, jnp.zeros_like(q_pair))
        s_prev[t] = _dot(qm, kt_blocks[b][sel[j]])
        s_cur[t] = _dot(qm, kt_blocks[b + 1][sel[j]])

    def softmax(t):
        b, j = jobs[t]
        s = jnp.where(upper, s_prev[t], s_cur[t])
        if b == 0:
            s = jnp.where(no_prev, NEG_BIG, s)
        sink = sink_ref[j]
        m = jnp.maximum(jnp.max(s, axis=-1, keepdims=True), sink)
        p = jnp.exp(s - m)
        denom[t] = jnp.sum(p, axis=-1, keepdims=True) + jnp.exp(sink - m)
        p_prev[t] = jnp.where(upper, p, 0.0).astype(BF16)
        p_cur[t] = jnp.where(upper, 0.0, p).astype(BF16)

    def values(t):
        b, j = jobs[t]
        outs[t] = (_dot(p_prev[t], v_blocks[b][sel[j]]), _dot(p_cur[t], v_blocks[b + 1][sel[j]]))

    def finish(t):
        b, j = jobs[t]
        outs[t] = (outs[t][0] + outs[t][1]) / denom[t]
        if j % 2 == 1:
            o_ref[rows[b], (j // 2) * LANES:(j // 2 + 1) * LANES] = jnp.where(
                low, outs[t - 1], outs[t]).astype(o_ref.dtype)

    for stage in (scores, softmax, values, finish):
        for t in range(len(jobs)):
            stage(t)


def _swa(qa, ka, va, sinks, batch, seq):
    n, qw = qa.shape
    kw = ka.shape[1]
    n_heads = qw // HEAD_DIM
    blk = WINDOW
    per_step = min(SWA_BLOCKS_PER_STEP, seq // blk)
    assert seq % (per_step * blk) == 0
    rows = per_step * blk
    ns = seq // rows
    cur = lambda b, i: (b * ns + i, 0)
    prev = lambda b, i: (b * ns * per_step + jnp.maximum(i * per_step - 1, 0), 0)
    return pl.pallas_call(
        functools.partial(_swa_kernel, n_heads, blk),
        grid=(batch, ns),
        in_specs=[pl.BlockSpec((rows, qw), cur),
                  pl.BlockSpec((rows, kw), cur), pl.BlockSpec((blk, kw), prev),
                  pl.BlockSpec((rows, kw), cur), pl.BlockSpec((blk, kw), prev),
                  pl.BlockSpec(memory_space=pltpu.SMEM)],
        out_specs=pl.BlockSpec((rows, qw), cur),
        out_shape=jax.ShapeDtypeStruct((n, qw), BF16),
        compiler_params=_cparams("parallel", "parallel"),
        name="swa",
    )(qa, ka, ka, va, va, sinks.astype(F32))


def _sb_kernel(q_ref, k_ref, v_ref, tri_ref, o_ref):
    i = pl.program_id(2)
    qb, width = q_ref.shape
    n_heads = 2 * (width // LANES)
    lane = lax.broadcasted_iota(jnp.int32, (qb, LANES), 1)
    low = lane < HEAD_DIM
    scale = jnp.asarray(HEAD_DIM ** -0.5, q_ref.dtype)
    q_heads = []
    for h in range(n_heads):
        q_pair = q_ref[:, (h // 2) * LANES:(h // 2 + 1) * LANES]
        q_heads.append(jnp.where(low if h % 2 == 0 else ~low, q_pair, jnp.zeros_like(q_pair)) * scale)
    tri = tri_ref[...]
    strict = (lax.broadcasted_iota(jnp.int32, (qb, qb), 1)
              < lax.broadcasted_iota(jnp.int32, (qb, qb), 0))

    def step(chunks, carries, accs):
        carries, accs = list(carries), list(accs)
        jobs = [(ci, h) for ci in range(len(chunks)) for h in range(n_heads)]
        z, log_beta, fail, fail_sum, later, w = {}, {}, {}, {}, {}, {}

        def cols(ref, ci, h):
            start = pl.multiple_of(chunks[ci][0] * qb, qb)
            return ref[pl.ds(start, qb), (h // 2) * LANES:(h // 2 + 1) * LANES]

        def scores(j):
            ci, h = jobs[j]
            z[j] = _dot_nt(q_heads[h], cols(k_ref, ci, h))

        def softplus(j):
            sp = jnp.maximum(z[j], 0.0) + jnp.log(1.0 + jnp.exp(-jnp.abs(z[j])))
            log_beta[j] = z[j] - sp
            spm = jnp.where(strict, sp, 0.0) if chunks[jobs[j][0]][1] else sp
            fail[j] = spm.astype(BF16)
            fail_sum[j] = jnp.sum(spm, axis=-1, keepdims=True)

        def suffix_sums(j):
            later[j] = _dot(fail[j], tri)

        def weights(j):
            ci, h = jobs[j]
            wj = jnp.exp(log_beta[j] - (later[j] + carries[h]))
            w[j] = (jnp.where(strict, wj, 0.0) if chunks[ci][1] else wj).astype(BF16)
            carries[h] = carries[h] + fail_sum[j]

        def values(j):
            ci, h = jobs[j]
            accs[h] = accs[h] + _dot(w[j], cols(v_ref, ci, h))

        for stage in (scores, softplus, suffix_sums, weights, values):
            for j in range(len(jobs)):
                stage(j)
        return tuple(carries), tuple(accs)

    carries = (jnp.zeros((qb, 1), F32),) * n_heads
    accs = (jnp.zeros((qb, LANES), F32),) * n_heads
    carries, accs = lax.cond(i == 0,
                             lambda: step([(i, True)], carries, accs),
                             lambda: step([(i, True), (i - 1, False)], carries, accs))

    def cond(state):
        t, carries, _ = state
        lowest = functools.reduce(jnp.minimum, [jnp.min(c) for c in carries])
        return (t < i - 1) & (lowest <= SB_UNDERFLOW)

    def body(state):
        t, carries, accs = state
        carries, accs = step([(i - 2 - t, False)], carries, accs)
        return t + 1, carries, accs

    _, _, accs = lax.while_loop(cond, body, (jnp.int32(0), carries, accs))
    for p in range(width // LANES):
        o_ref[:, p * LANES:(p + 1) * LANES] = jnp.where(low, accs[2 * p], accs[2 * p + 1]).astype(o_ref.dtype)


def _stick_breaking(qb, kb, vb, batch, seq):
    n, w = qb.shape
    blk = min(SB_QBLK, seq)
    nq = seq // blk
    cols = SB_PAIRS_PER_STEP * LANES
    tri = jnp.asarray(np.tril(np.ones((blk, blk), np.float32), -1), BF16)
    return pl.pallas_call(
        _sb_kernel,
        grid=(batch, w // cols, nq),
        in_specs=[pl.BlockSpec((blk, cols), lambda b, p, i: (b * nq + i, p)),
                  pl.BlockSpec((seq, cols), lambda b, p, i: (b, p)),
                  pl.BlockSpec((seq, cols), lambda b, p, i: (b, p)),
                  _resident((blk, blk))],
        out_specs=pl.BlockSpec((blk, cols), lambda b, p, i: (b * nq + i, p)),
        out_shape=jax.ShapeDtypeStruct((n, w), BF16),
        compiler_params=_cparams("parallel", "parallel", "arbitrary"),
        name="stick_breaking",
    )(qb, kb, vb, tri)


def _gelu_tanh(x):
    return 0.5 * x * (1.0 + jnp.tanh(np.sqrt(2.0 / np.pi) * (x + 0.044715 * (x * x * x))))


def _lru_kernel(x_ref, g_ref, cw_ref, cb_ref, wri_ref, bri_ref, lam_ref, o_ref, tail_ref, h_ref):
    j = pl.program_id(1)
    ts, width = x_ref.shape

    @pl.when(j == 0)
    def _():
        tail_ref[...] = jnp.zeros_like(tail_ref)
        h_ref[...] = jnp.zeros_like(h_ref)

    x = x_ref[...].astype(F32)
    tail = tail_ref[...]
    row8 = lax.broadcasted_iota(jnp.int32, (SUBLANES, width), 0)
    y = x * cw_ref[CONV_W - 1:CONV_W, :] + cb_ref[...]
    for d in range(1, CONV_W):
        xs = pltpu.roll(x, d, 0)
        top = jnp.where(row8 < d, pltpu.roll(tail, d, 0), xs[:SUBLANES])
        xs = jnp.concatenate([top, xs[SUBLANES:]], axis=0)
        y = y + xs * cw_ref[CONV_W - 1 - d:CONV_W - d, :]
    tail_ref[...] = x[ts - SUBLANES:]

    ri = _dot(y.astype(BF16), wri_ref[...]) + bri_ref[...]
    r = jax.nn.sigmoid(ri[:, :width])
    gate_i = jax.nn.sigmoid(ri[:, width:])
    lam = lam_ref[...]
    softplus_neg_lam = jnp.maximum(-lam, 0.0) + jnp.log(1.0 + jnp.exp(-jnp.abs(lam)))
    a = jnp.exp2(r * ((-LRU_C * np.log2(np.e)) * softplus_neg_lam))
    u = jnp.sqrt(1.0 - a * a) * (gate_i * y)

    grouped = (ts // SUBLANES, SUBLANES, width)
    a, u = a.reshape(grouped), u.reshape(grouped)
    in_group = lax.broadcasted_iota(jnp.int32, grouped, 1)
    d = 1
    while d < SUBLANES:
        keep = in_group >= d
        a_sh = jnp.where(keep, pltpu.roll(a, d, 1), 1.0)
        u_sh = jnp.where(keep, pltpu.roll(u, d, 1), 0.0)
        u = a * u_sh + u
        a = a * a_sh
        d *= 2
    a, u = a.reshape(ts, width), u.reshape(ts, width)
    state = h_ref[...]
    groups = []
    for g in range(ts // SUBLANES):
        rows = slice(g * SUBLANES, (g + 1) * SUBLANES)
        hg = u[rows] + a[rows] * state
        groups.append(hg)
        state = hg[SUBLANES - 1:SUBLANES, :]
    h = jnp.concatenate(groups, axis=0)
    h_ref[...] = state
    o_ref[...] = (h * _gelu_tanh(g_ref[...].astype(F32))).astype(o_ref.dtype)


def _block_diag(w):
    heads, blk, _ = w.shape
    eye = jnp.eye(heads, dtype=w.dtype)
    return jnp.einsum('hij,hg->higj', w, eye).reshape(heads * blk, heads * blk)


def _rglru(xc, gc, conv_w, conv_b, w_r, b_r, w_i, b_i, lam, batch, seq):
    n, width = xc.shape
    ts = min(LRU_CHUNK, seq)
    nc = seq // ts
    wri = jnp.concatenate([_block_diag(w_r), _block_diag(w_i)], axis=1).astype(BF16)
    bri = jnp.concatenate([b_r, b_i]).astype(F32).reshape(1, 2 * width)
    blk = lambda b, j: (b * nc + j, 0)
    return pl.pallas_call(
        _lru_kernel,
        grid=(batch, nc),
        in_specs=[pl.BlockSpec((ts, width), blk), pl.BlockSpec((ts, width), blk),
                  _resident((CONV_W, width)), _resident((1, width)),
                  _resident((width, 2 * width)), _resident((1, 2 * width)),
                  _resident((1, width))],
        out_specs=pl.BlockSpec((ts, width), blk),
        out_shape=jax.ShapeDtypeStruct((n, width), BF16),
        scratch_shapes=[pltpu.VMEM((SUBLANES, width), F32), pltpu.VMEM((1, width), F32)],
        compiler_params=_cparams("parallel", "arbitrary"),
        name="rglru",
    )(xc, gc, conv_w.astype(F32), conv_b.astype(F32).reshape(1, width), wri, bri,
      lam.astype(F32).reshape(1, width))


def _merge_kernel(x_ref, oa_ref, ob_ref, oc_ref, gt_ref, bg_ref, wa_ref, wb_ref, wc_ref, wo_ref, o_ref):
    tm, d = x_ref.shape
    branches = ((oa_ref, wa_ref), (ob_ref, wb_ref), (oc_ref, wc_ref))
    halves = [slice(r * (tm // 2), (r + 1) * (tm // 2)) for r in range(2)]
    proj = [[_dot(b_ref[rows, :], w_ref[...]) for b_ref, w_ref in branches] for rows in halves]
    for rows, terms in zip(halves, proj):
        merged = None
        for idx, term in enumerate(terms):
            cols = slice(idx * d, (idx + 1) * d)
            gate = jax.nn.sigmoid(gt_ref[rows, cols].astype(F32) + bg_ref[:, cols])
            merged = gate * term if merged is None else merged + gate * term
        o_ref[rows, :] = x_ref[rows, :] + _dot(merged.astype(BF16), wo_ref[...])


def _merge(x2, o_a, o_b, o_c, gates, b_gate, wa, wb, wc, wo):
    n, d = x2.shape
    tm = min(ROW_TILE, n)
    row = lambda i: (i, 0)
    return pl.pallas_call(
        _merge_kernel,
        grid=(n // tm,),
        in_specs=[pl.BlockSpec((tm, d), row),
                  pl.BlockSpec((tm, o_a.shape[1]), row), pl.BlockSpec((tm, o_b.shape[1]), row),
                  pl.BlockSpec((tm, o_c.shape[1]), row), pl.BlockSpec((tm, gates.shape[1]), row),
                  _resident((1, gates.shape[1])),
                  _resident(wa.shape), _resident(wb.shape), _resident(wc.shape), _resident(wo.shape)],
        out_specs=pl.BlockSpec((tm, d), row),
        out_shape=jax.ShapeDtypeStruct((n, d), F32),
        compiler_params=_cparams("parallel"),
        name="merge",
    )(x2, o_a, o_b, o_c, gates, b_gate.astype(F32).reshape(1, -1), wa, wb, wc, wo)


def _swiglu_tile(h, wg_ref, wu_ref, wd_ref, ff_chunk):
    d_ff = wg_ref.shape[-1]
    starts = list(range(0, d_ff, ff_chunk))

    def gate_up(c):
        return _dot(h, wg_ref[:, c:c + ff_chunk]), _dot(h, wu_ref[:, c:c + ff_chunk])

    acc = None
    pending = gate_up(starts[0])
    for i, c in enumerate(starts):
        g, u = pending
        if i + 1 < len(starts):
            pending = gate_up(starts[i + 1])
        act = (g * jax.nn.sigmoid(g) * u).astype(BF16)
        part = _dot(act, wd_ref[c:c + ff_chunk, :])
        acc = part if acc is None else acc + part
    return acc


def _ffn_chunk(d_ff):
    for c in (512, 896, 256, 128):
        if d_ff % c == 0:
            return c
    return d_ff


def _dense_ffn_kernel(ff_chunk, x_ref, g_ref, wg_ref, wu_ref, wd_ref, o_ref):
    x = x_ref[...]
    h = _rms(x, g_ref[...]).astype(BF16)
    o_ref[...] = x + _swiglu_tile(h, wg_ref, wu_ref, wd_ref, ff_chunk)


def _dense_ffn(x2, gain, wg, wu, wd):
    n, d = x2.shape
    tm = min(ROW_TILE, n)
    return pl.pallas_call(
        functools.partial(_dense_ffn_kernel, _ffn_chunk(wg.shape[1])),
        grid=(n // tm,),
        in_specs=[pl.BlockSpec((tm, d), lambda i: (i, 0)), _resident((1, d)),
                  _resident(wg.shape), _resident(wu.shape), _resident(wd.shape)],
        out_specs=pl.BlockSpec((tm, d), lambda i: (i, 0)),
        out_shape=jax.ShapeDtypeStruct((n, d), F32),
        compiler_params=_cparams("parallel"),
        name="dense_ffn",
    )(x2, gain, wg, wu, wd)


def _router_kernel(n_experts, stride, x_ref, g_ref, wr_ref, tril_ref, h_ref, idx_ref, wt_ref, total_ref, count_ref):
    h = _rms(x_ref[...], g_ref[...])
    h_ref[...] = h.astype(h_ref.dtype)
    w = wr_ref[...]
    h1 = h.astype(BF16)
    h2 = (h - h1.astype(F32)).astype(BF16)
    h3 = (h - h1.astype(F32) - h2.astype(F32)).astype(BF16)
    a, b, c = _dot(h1, w), _dot(h2, w), _dot(h3, w)

    def term(x, k):
        return x if k == 0 else pltpu.roll(x, LANES - k * stride, 1)

    logits = ((term(a, 0) + (term(a, 1) + term(b, 0)))
              + (term(b, 1) + term(a, 2) + term(c, 0)))
    lane = lax.broadcasted_iota(jnp.int32, logits.shape, 1)
    logits = jnp.where(lane < n_experts, logits, NEG_BIG)
    m1 = jnp.max(logits, axis=-1, keepdims=True)
    i1 = jnp.min(jnp.where(logits == m1, lane, LANES), axis=-1, keepdims=True)
    rest = jnp.where(lane == i1, NEG_BIG, logits)
    m2 = jnp.max(rest, axis=-1, keepdims=True)
    i2 = jnp.min(jnp.where(rest == m2, lane, LANES), axis=-1, keepdims=True)
    e2 = jnp.exp(m2 - m1)
    wt1 = 1.0 / (1.0 + e2)
    wt2 = e2 / (1.0 + e2)
    wt_ref[...] = jnp.where(lane == 0, wt1, jnp.where(lane == 1, wt2, 0.0))

    @pl.when(pl.program_id(0) == 0)
    def _():
        count_ref[...] = jnp.zeros_like(count_ref)

    chosen = jnp.where((lane == i1) | (lane == i2), 1.0, 0.0)
    before = _dot(tril_ref[...], chosen.astype(BF16)) + count_ref[...]
    r1 = jnp.sum(jnp.where(lane == i1, before, 0.0), axis=-1, keepdims=True).astype(jnp.int32)
    r2 = jnp.sum(jnp.where(lane == i2, before, 0.0), axis=-1, keepdims=True).astype(jnp.int32)
    count_ref[...] = count_ref[...] + jnp.sum(chosen, axis=0, keepdims=True)
    total_ref[...] = count_ref[...]
    route = jnp.where(lane == 0, i1, jnp.where(lane == 1, i2,
                      jnp.where(lane == 2, r1, jnp.where(lane == 3, r2, 0))))
    idx_ref[...] = jnp.transpose(route)[:idx_ref.shape[0], :]


def _router(x2, gain, w_router):
    n, d = x2.shape
    n_experts = w_router.shape[1]
    tm = min(ROW_TILE, n)
    stride = SUBLANES * (-(-n_experts // SUBLANES))
    assert 3 * stride <= LANES
    w = w_router.astype(F32)
    w1 = w.astype(BF16)
    w2 = (w - w1.astype(F32)).astype(BF16)
    w3 = (w - w1.astype(F32) - w2.astype(F32)).astype(BF16)
    wr = jnp.zeros((d, LANES), BF16)
    for k, wk in enumerate((w1, w2, w3)):
        wr = wr.at[:, k * stride:k * stride + n_experts].set(wk)
    tril = jnp.asarray(np.tril(np.ones((tm, tm), np.float32), -1), BF16)
    row = lambda i: (i, 0)
    return pl.pallas_call(
        functools.partial(_router_kernel, n_experts, stride),
        grid=(n // tm,),
        in_specs=[pl.BlockSpec((tm, d), row), _resident((1, d)), _resident((d, LANES)), _resident((tm, tm))],
        out_specs=[pl.BlockSpec((tm, d), row), pl.BlockSpec((SUBLANES, tm), lambda i: (0, i)),
                   pl.BlockSpec((tm, LANES), row), pl.BlockSpec((1, LANES), lambda i: (0, 0))],
        out_shape=[jax.ShapeDtypeStruct((n, d), BF16), jax.ShapeDtypeStruct((SUBLANES, n), jnp.int32),
                   jax.ShapeDtypeStruct((n, LANES), F32), jax.ShapeDtypeStruct((1, LANES), F32)],
        scratch_shapes=[pltpu.VMEM((1, LANES), F32)],
        compiler_params=_cparams("arbitrary"),
        name="router",
    )(x2, gain, wr, tril)


def _moe_kernel(ff_chunk, te_ref, tv_ref, x_ref, wg_ref, wu_ref, wd_ref, o_ref):
    t = pl.program_id(0)

    @pl.when(tv_ref[t] > 0)
    def _():
        y = _swiglu_tile(x_ref[...], wg_ref.at[0], wu_ref.at[0], wd_ref.at[0], ff_chunk)
        o_ref[...] = y.astype(o_ref.dtype)

    @pl.when(tv_ref[t] == 0)
    def _():
        o_ref[...] = jnp.zeros_like(o_ref)


def _moe_experts(xs, tile_expert, tile_valid, wg, wu, wd):
    p, d = xs.shape
    d_ff = wg.shape[2]
    tm = MOE_TILE
    single = pl.Buffered(1)
    grid_spec = pltpu.PrefetchScalarGridSpec(
        num_scalar_prefetch=2,
        grid=(p // tm,),
        in_specs=[pl.BlockSpec((tm, d), lambda t, te, tv: (t, 0)),
                  pl.BlockSpec((1, d, d_ff), lambda t, te, tv: (te[t], 0, 0), pipeline_mode=single),
                  pl.BlockSpec((1, d, d_ff), lambda t, te, tv: (te[t], 0, 0), pipeline_mode=single),
                  pl.BlockSpec((1, d_ff, d), lambda t, te, tv: (te[t], 0, 0), pipeline_mode=single)],
        out_specs=pl.BlockSpec((tm, d), lambda t, te, tv: (t, 0)),
    )
    return pl.pallas_call(
        functools.partial(_moe_kernel, _ffn_chunk(d_ff)),
        grid_spec=grid_spec,
        out_shape=jax.ShapeDtypeStruct((p, d), BF16),
        compiler_params=_cparams("arbitrary"),
        name="moe_experts",
    )(tile_expert, tile_valid, xs, wg, wu, wd)


def _combine_kernel(x_ref, wt_ref, y0_ref, y1_ref, o_ref):
    wt = wt_ref[...]
    o_ref[...] = x_ref[...] + (wt[:, 0:1] * y0_ref[...].astype(F32) + wt[:, 1:2] * y1_ref[...].astype(F32))


def _combine(x2, wt, y0, y1):
    n, d = x2.shape
    tm = min(ROW_TILE, n)
    row = lambda i: (i, 0)
    return pl.pallas_call(
        _combine_kernel,
        grid=(n // tm,),
        in_specs=[pl.BlockSpec((tm, d), row), pl.BlockSpec((tm, LANES), row),
                  pl.BlockSpec((tm, d), row), pl.BlockSpec((tm, d), row)],
        out_specs=pl.BlockSpec((tm, d), row),
        out_shape=jax.ShapeDtypeStruct((n, d), F32),
        compiler_params=_cparams("parallel"),
        name="moe_combine",
    )(x2, wt, y0, y1)


def _moe_ffn(x2, gain, w_router, expert_weights, layer):
    n, d = x2.shape
    n_experts = w_router.shape[1]
    tm = MOE_TILE
    h, route, top_w, totals = _router(x2, gain, w_router)

    sizes = totals[0, :n_experts].astype(jnp.int32)
    padded = ((sizes + tm - 1) // tm) * tm
    starts = jnp.cumsum(padded) - padded
    pos = []
    for k in range(TOP_K):
        expert, rank = route[k], route[TOP_K + k]
        base = jnp.zeros_like(rank)
        for e in range(n_experts):
            base = base + jnp.where(expert == e, starts[e], 0)
        pos.append(rank + base)
    n_tiles = (n * TOP_K) // tm + n_experts
    p = n_tiles * tm
    token = jnp.tile(jnp.arange(n, dtype=jnp.int32), TOP_K)
    row_src = jnp.zeros((p,), jnp.int32).at[jnp.concatenate(pos)].set(
        token, unique_indices=True, mode="promise_in_bounds")
    tile_start = jnp.arange(n_tiles, dtype=jnp.int32) * tm
    ends = starts + padded
    tile_expert = jnp.minimum(jnp.sum((tile_start[:, None] >= ends[None, :]).astype(jnp.int32), axis=1),
                              n_experts - 1).astype(jnp.int32)
    tile_valid = (tile_start < ends[-1]).astype(jnp.int32)

    xs = h.at[row_src].get(mode="promise_in_bounds")
    wg, wu, wd = (_to_bf16(w, layer, after=row_src) for w in expert_weights)
    ys = _moe_experts(xs, tile_expert, tile_valid, wg, wu, wd)
    y0 = ys.at[pos[0]].get(mode="promise_in_bounds")
    y1 = ys.at[pos[1]].get(mode="promise_in_bounds")
    return _combine(x2, top_w, y0, y1)


def kernel(x, attn_norm, w_in, b_gate, q_norm, k_norm, sinks, conv_w, conv_b, lru_w_r, lru_b_r, lru_w_i, lru_b_i, lru_lambda, w_proj_a, w_proj_b, w_proj_c, w_out, ffn_norm, w_ffn_gate, w_ffn_up, w_ffn_down, w_router, w_exp_gate, w_exp_up, w_exp_down):
    batch, seq, d = x.shape
    depth = w_in.shape[0]
    swa_q = w_proj_a.shape[1]
    sb_w = w_proj_b.shape[1]
    lru_w = w_proj_c.shape[1]
    swa_kv = SWA_KV_HEADS * HEAD_DIM
    widths = (swa_q, swa_kv, swa_kv, sb_w, sb_w, sb_w, lru_w, lru_w, N_BRANCHES * d)
    assert sum(widths) == w_in.shape[2]
    assert seq % SB_QBLK == 0 and seq % WINDOW == 0

    x2 = x.reshape(batch * seq, d).astype(F32)
    for l in range(depth):
        gain = attn_norm[l].astype(F32).reshape(1, d)
        qa, ka, va, qb, kb, vb, xc, gc, gates = _in_proj(x2, gain, _to_bf16(w_in, l), widths,
                                                         q_norm[l], k_norm[l])
        o_a = _swa(qa, ka, va, sinks[l], batch, seq)
        o_b = _stick_breaking(qb, kb, vb, batch, seq)
        o_c = _rglru(xc, gc, conv_w[l], conv_b[l], lru_w_r[l], lru_b_r[l], lru_w_i[l], lru_b_i[l],
                     lru_lambda[l], batch, seq)
        x2 = _merge(x2, o_a, o_b, o_c, gates, b_gate[l],
                    _to_bf16(w_proj_a, l), _to_bf16(w_proj_b, l), _to_bf16(w_proj_c, l),
                    _to_bf16(w_out, l))
        fgain = ffn_norm[l].astype(F32).reshape(1, d)
        j = l // 2
        if l % 2 == 0:
            x2 = _dense_ffn(x2, fgain, _to_bf16(w_ffn_gate, j), _to_bf16(w_ffn_up, j),
                            _to_bf16(w_ffn_down, j))
        else:
            x2 = _moe_ffn(x2, fgain, w_router[j], (w_exp_gate, w_exp_up, w_exp_down), j)
    return x2.reshape(batch, seq, d).astype(x.dtype)
```

```python
import functools

import numpy as np
import jax
import jax.numpy as jnp
from jax import lax
from jax.experimental import pallas as pl
from jax.experimental.pallas import tpu as pltpu

F32 = jnp.float32
BF16 = jnp.bfloat16

HEAD_DIM = 64
SWA_KV_HEADS = 2
WINDOW = 128
LRU_HEADS = 8
LRU_C = 8.0
CONV_W = 4
N_BRANCHES = 3
TOP_K = 2
EPS = 1e-6

LANES = 128
SUBLANES = 8
VMEM_LIMIT = 56 * 1024 * 1024
NEG_BIG = -1e30

SWA_BLOCKS_PER_STEP = 4
SB_QBLK = 256
SB_PAIRS_PER_STEP = 4
SB_UNDERFLOW = 105.0
LRU_CHUNK = 512
ROW_TILE = 512
MOE_TILE = 512


def _cparams(*sem):
    return pltpu.CompilerParams(dimension_semantics=sem, vmem_limit_bytes=VMEM_LIMIT)


def _resident(shape):
    nd = len(shape)
    return pl.BlockSpec(shape, lambda *_: (0,) * nd, pipeline_mode=pl.Buffered(1))


def _dot(a, b):
    return jnp.dot(a, b, preferred_element_type=F32)


def _dot_nt(a, b):
    return lax.dot_general(a, b, (((1,), (1,)), ((), ())), preferred_element_type=F32)


def _split_dot(a, b_bf16):
    hi = a.astype(BF16)
    lo = (a - hi.astype(F32)).astype(BF16)
    return _dot(hi, b_bf16) + _dot(lo, b_bf16)


def _rms(x, gain):
    ms = jnp.mean(x * x, axis=-1, keepdims=True)
    return x * lax.rsqrt(ms + EPS) * gain


CAST_BLOCK_BYTES = 4 * 1024 * 1024


def _cast_kernel(after_ref, x_ref, o_ref):
    del after_ref
    o_ref[...] = x_ref[...].astype(o_ref.dtype)


def _to_bf16(w, layer, after=None):
    shape = w.shape[1:]
    cols = shape[-1]
    rows = int(np.prod(shape[:-1]))
    tr = rows
    while tr * cols * 4 > CAST_BLOCK_BYTES and tr % 32 == 0:
        tr //= 2
    token = jnp.zeros((1,), jnp.int32) if after is None else after.reshape(-1)[:1]
    out = pl.pallas_call(
        _cast_kernel,
        grid=(rows // tr,),
        in_specs=[pl.BlockSpec(memory_space=pltpu.SMEM),
                  pl.BlockSpec((None, tr, cols), lambda i: (layer, i, 0))],
        out_specs=pl.BlockSpec((tr, cols), lambda i: (i, 0)),
        out_shape=jax.ShapeDtypeStruct((rows, cols), BF16),
        compiler_params=_cparams("parallel"),
        name="cast_bf16",
    )(token, w.reshape(w.shape[0], rows, cols))
    return out.reshape(shape)


def _head_norm(xf, bd, gain):
    ms = _dot((xf * xf).astype(BF16), bd)
    return xf * lax.rsqrt(ms + EPS) * gain


def _in_proj_kernel(widths, x_ref, g_ref, w_ref, bdq_ref, bdk_ref, qg_ref, kg_ref, *out_refs):
    h = _rms(x_ref[...], g_ref[...]).astype(BF16)
    normed = ((bdq_ref, qg_ref), (bdk_ref, kg_ref))
    offs = [sum(widths[:i]) for i in range(len(widths))]
    accs = [_dot(h, w_ref[:, offs[i]:offs[i] + widths[i]]) for i in range(len(normed))]
    for idx in range(len(normed), len(widths)):
        ref, width, off = out_refs[idx], widths[idx], offs[idx]
        for c in range(0, width, 1024):
            cw = min(1024, width - c)
            ref[:, c:c + cw] = _dot(h, w_ref[:, off + c:off + c + cw]).astype(ref.dtype)
    for idx, (bd_ref, gain_ref) in enumerate(normed):
        out_refs[idx][...] = _head_norm(accs[idx], bd_ref[...], gain_ref[...]).astype(out_refs[idx].dtype)


def _block_diag_mean(width):
    idx = np.arange(width) // HEAD_DIM
    return jnp.asarray((idx[:, None] == idx[None, :]).astype(np.float32) / HEAD_DIM, BF16)


def _in_proj(x2, gain, w_bf16, widths, q_norm, k_norm):
    n, d = x2.shape
    tm = min(ROW_TILE, n)
    qw, kw = widths[0], widths[1]
    qg = (jnp.tile(q_norm.astype(F32), qw // HEAD_DIM) * (HEAD_DIM ** -0.5)).reshape(1, qw)
    kg = jnp.tile(k_norm.astype(F32), kw // HEAD_DIM).reshape(1, kw)
    return pl.pallas_call(
        functools.partial(_in_proj_kernel, widths),
        grid=(n // tm,),
        in_specs=[pl.BlockSpec((tm, d), lambda i: (i, 0)),
                  _resident((1, d)),
                  _resident(w_bf16.shape),
                  _resident((qw, qw)), _resident((kw, kw)),
                  _resident((1, qw)), _resident((1, kw))],
        out_specs=[pl.BlockSpec((tm, w), lambda i: (i, 0)) for w in widths],
        out_shape=[jax.ShapeDtypeStruct((n, w), BF16) for w in widths],
        compiler_params=_cparams("parallel"),
        name="in_proj",
    )(x2, gain, w_bf16, _block_diag_mean(qw), _block_diag_mean(kw), qg, kg)


def _swa_kernel(n_heads, blk, q_ref, kc_ref, kp_ref, vc_ref, vp_ref, sink_ref, o_ref):
    n = pl.program_id(1)
    n_blocks = q_ref.shape[0] // blk
    group = n_heads // SWA_KV_HEADS

    upper = (lax.broadcasted_iota(jnp.int32, (blk, blk), 1)
             > lax.broadcasted_iota(jnp.int32, (blk, blk), 0))
    no_prev = upper & (n == 0)
    lane = lax.broadcasted_iota(jnp.int32, (blk, LANES), 1)
    low = lane < HEAD_DIM

    def swapped(t):
        return (t, pltpu.roll(t, HEAD_DIM, 1))

    rows = [slice(b * blk, (b + 1) * blk) for b in range(n_blocks)]
    k_blocks = [swapped(kp_ref[...])] + [swapped(kc_ref[r, :]) for r in rows]
    v_blocks = [swapped(vp_ref[...])] + [swapped(vc_ref[r, :]) for r in rows]
    kt_blocks = [tuple(jnp.transpose(t) for t in pair) for pair in k_blocks]

    jobs = [(b, j) for b in range(n_blocks) for j in range(n_heads)]
    sel = [(j // group + j % 2) % 2 for j in range(n_heads)]
    s_prev, s_cur, p_prev, p_cur, denom, outs = {}, {}, {}, {}, {}, {}

    def scores(t):
        b, j = jobs[t]
        q_pair = q_ref[rows[b], (j // 2) * LANES:(j // 2 + 1) * LANES]
        qm = jnp.where(low if j % 2 == 0 else ~low, q_pair, jnp.zeros_like(q_pair))
        s_prev[t] = _dot(qm, kt_blocks[b][sel[j]])
        s_cur[t] = _dot(qm, kt_blocks[b + 1][sel[j]])

    def softmax(t):
        b, j = jobs[t]
        s = jnp.where(upper, s_prev[t], s_cur[t])
        if b == 0:
            s = jnp.where(no_prev, NEG_BIG, s)
        sink = sink_ref[j]
        m = jnp.maximum(jnp.max(s, axis=-1, keepdims=True), sink)
        p = jnp.exp(s - m)
        denom[t] = jnp.sum(p, axis=-1, keepdims=True) + jnp.exp(sink - m)
        p_prev[t] = jnp.where(upper, p, 0.0).astype(BF16)
        p_cur[t] = jnp.where(upper, 0.0, p).astype(BF16)

    def values(t):
        b, j = jobs[t]
        outs[t] = (_dot(p_prev[t], v_blocks[b][sel[j]]), _dot(p_cur[t], v_blocks[b + 1][sel[j]]))

    def finish(t):
        b, j = jobs[t]
        outs[t] = (outs[t][0] + outs[t][1]) / denom[t]
        if j % 2 == 1:
            o_ref[rows[b], (j // 2) * LANES:(j // 2 + 1) * LANES] = jnp.where(
                low, outs[t - 1], outs[t]).astype(o_ref.dtype)

    for stage in (scores, softmax, values, finish):
        for t in range(len(jobs)):
            stage(t)


def _swa(qa, ka, va, sinks, batch, seq):
    n, qw = qa.shape
    kw = ka.shape[1]
    n_heads = qw // HEAD_DIM
    blk = WINDOW
    per_step = min(SWA_BLOCKS_PER_STEP, seq // blk)
    assert seq % (per_step * blk) == 0
    rows = per_step * blk
    ns = seq // rows
    cur = lambda b, i: (b * ns + i, 0)
    prev = lambda b, i: (b * ns * per_step + jnp.maximum(i * per_step - 1, 0), 0)
    return pl.pallas_call(
        functools.partial(_swa_kernel, n_heads, blk),
        grid=(batch, ns),
        in_specs=[pl.BlockSpec((rows, qw), cur),
                  pl.BlockSpec((rows, kw), cur), pl.BlockSpec((blk, kw), prev),
                  pl.BlockSpec((rows, kw), cur), pl.BlockSpec((blk, kw), prev),
                  pl.BlockSpec(memory_space=pltpu.SMEM)],
        out_specs=pl.BlockSpec((rows, qw), cur),
        out_shape=jax.ShapeDtypeStruct((n, qw), BF16),
        compiler_params=_cparams("parallel", "parallel"),
        name="swa",
    )(qa, ka, ka, va, va, sinks.astype(F32))


def _sb_kernel(q_ref, k_ref, v_ref, tri_ref, o_ref):
    i = pl.program_id(2)
    qb, width = q_ref.shape
    n_heads = 2 * (width // LANES)
    lane = lax.broadcasted_iota(jnp.int32, (qb, LANES), 1)
    low = lane < HEAD_DIM
    scale = jnp.asarray(HEAD_DIM ** -0.5, q_ref.dtype)
    q_heads = []
    for h in range(n_heads):
        q_pair = q_ref[:, (h // 2) * LANES:(h // 2 + 1) * LANES]
        q_heads.append(jnp.where(low if h % 2 == 0 else ~low, q_pair, jnp.zeros_like(q_pair)) * scale)
    tri = tri_ref[...]
    strict = (lax.broadcasted_iota(jnp.int32, (qb, qb), 1)
              < lax.broadcasted_iota(jnp.int32, (qb, qb), 0))

    def step(chunks, carries, accs):
        carries, accs = list(carries), list(accs)
        jobs = [(ci, h) for ci in range(len(chunks)) for h in range(n_heads)]
        z, log_beta, fail, fail_sum, later, w = {}, {}, {}, {}, {}, {}

        def cols(ref, ci, h):
            start = pl.multiple_of(chunks[ci][0] * qb, qb)
            return ref[pl.ds(start, qb), (h // 2) * LANES:(h // 2 + 1) * LANES]

        def scores(j):
            ci, h = jobs[j]
            z[j] = _dot_nt(q_heads[h], cols(k_ref, ci, h))

        def softplus(j):
            sp = jnp.maximum(z[j], 0.0) + jnp.log(1.0 + jnp.exp(-jnp.abs(z[j])))
            log_beta[j] = z[j] - sp
            spm = jnp.where(strict, sp, 0.0) if chunks[jobs[j][0]][1] else sp
            fail[j] = spm.astype(BF16)
            fail_sum[j] = jnp.sum(spm, axis=-1, keepdims=True)

        def suffix_sums(j):
            later[j] = _dot(fail[j], tri)

        def weights(j):
            ci, h = jobs[j]
            wj = jnp.exp(log_beta[j] - (later[j] + carries[h]))
            w[j] = (jnp.where(strict, wj, 0.0) if chunks[ci][1] else wj).astype(BF16)
            carries[h] = carries[h] + fail_sum[j]

        def values(j):
            ci, h = jobs[j]
            accs[h] = accs[h] + _dot(w[j], cols(v_ref, ci, h))

        for stage in (scores, softplus, suffix_sums, weights, values):
            for j in range(len(jobs)):
                stage(j)
        return tuple(carries), tuple(accs)

    carries = (jnp.zeros((qb, 1), F32),) * n_heads
    accs = (jnp.zeros((qb, LANES), F32),) * n_heads
    carries, accs = lax.cond(i == 0,
                             lambda: step([(i, True)], carries, accs),
                             lambda: step([(i, True), (i - 1, False)], carries, accs))

    def cond(state):
        t, carries, _ = state
        lowest = functools.reduce(jnp.minimum, [jnp.min(c) for c in carries])
        return (t < i - 1) & (lowest <= SB_UNDERFLOW)

    def body(state):
        t, carries, accs = state
        carries, accs = step([(i - 2 - t, False)], carries, accs)
        return t + 1, carries, accs

    _, _, accs = lax.while_loop(cond, body, (jnp.int32(0), carries, accs))
    for p in range(width // LANES):
        o_ref[:, p * LANES:(p + 1) * LANES] = jnp.where(low, accs[2 * p], accs[2 * p + 1]).astype(o_ref.dtype)


def _stick_breaking(qb, kb, vb, batch, seq):
    n, w = qb.shape
    blk = min(SB_QBLK, seq)
    nq = seq // blk
    cols = SB_PAIRS_PER_STEP * LANES
    tri = jnp.asarray(np.tril(np.ones((blk, blk), np.float32), -1), BF16)
    return pl.pallas_call(
        _sb_kernel,
        grid=(batch, w // cols, nq),
        in_specs=[pl.BlockSpec((blk, cols), lambda b, p, i: (b * nq + i, p)),
                  pl.BlockSpec((seq, cols), lambda b, p, i: (b, p)),
                  pl.BlockSpec((seq, cols), lambda b, p, i: (b, p)),
                  _resident((blk, blk))],
        out_specs=pl.BlockSpec((blk, cols), lambda b, p, i: (b * nq + i, p)),
        out_shape=jax.ShapeDtypeStruct((n, w), BF16),
        compiler_params=_cparams("parallel", "parallel", "arbitrary"),
        name="stick_breaking",
    )(qb, kb, vb, tri)


def _gelu_tanh(x):
    return 0.5 * x * (1.0 + jnp.tanh(np.sqrt(2.0 / np.pi) * (x + 0.044715 * (x * x * x))))


def _lru_kernel(x_ref, g_ref, cw_ref, cb_ref, wri_ref, bri_ref, lam_ref, o_ref, tail_ref, h_ref):
    j = pl.program_id(1)
    ts, width = x_ref.shape

    @pl.when(j == 0)
    def _():
        tail_ref[...] = jnp.zeros_like(tail_ref)
        h_ref[...] = jnp.zeros_like(h_ref)

    x = x_ref[...].astype(F32)
    tail = tail_ref[...]
    row8 = lax.broadcasted_iota(jnp.int32, (SUBLANES, width), 0)
    y = x * cw_ref[CONV_W - 1:CONV_W, :] + cb_ref[...]
    for d in range(1, CONV_W):
        xs = pltpu.roll(x, d, 0)
        top = jnp.where(row8 < d, pltpu.roll(tail, d, 0), xs[:SUBLANES])
        xs = jnp.concatenate([top, xs[SUBLANES:]], axis=0)
        y = y + xs * cw_ref[CONV_W - 1 - d:CONV_W - d, :]
    tail_ref[...] = x[ts - SUBLANES:]

    ri = _dot(y.astype(BF16), wri_ref[...]) + bri_ref[...]
    r = jax.nn.sigmoid(ri[:, :width])
    gate_i = jax.nn.sigmoid(ri[:, width:])
    lam = lam_ref[...]
    softplus_neg_lam = jnp.maximum(-lam, 0.0) + jnp.log(1.0 + jnp.exp(-jnp.abs(lam)))
    a = jnp.exp2(r * ((-LRU_C * np.log2(np.e)) * softplus_neg_lam))
    u = jnp.sqrt(1.0 - a * a) * (gate_i * y)

    grouped = (ts // SUBLANES, SUBLANES, width)
    a, u = a.reshape(grouped), u.reshape(grouped)
    in_group = lax.broadcasted_iota(jnp.int32, grouped, 1)
    d = 1
    while d < SUBLANES:
        keep = in_group >= d
        a_sh = jnp.where(keep, pltpu.roll(a, d, 1), 1.0)
        u_sh = jnp.where(keep, pltpu.roll(u, d, 1), 0.0)
        u = a * u_sh + u
        a = a * a_sh
        d *= 2
    a, u = a.reshape(ts, width), u.reshape(ts, width)
    state = h_ref[...]
    groups = []
    for g in range(ts // SUBLANES):
        rows = slice(g * SUBLANES, (g + 1) * SUBLANES)
        hg = u[rows] + a[rows] * state
        groups.append(hg)
        state = hg[SUBLANES - 1:SUBLANES, :]
    h = jnp.concatenate(groups, axis=0)
    h_ref[...] = state
    o_ref[...] = (h * _gelu_tanh(g_ref[...].astype(F32))).astype(o_ref.dtype)


def _block_diag(w):
    heads, blk, _ = w.shape
    eye = jnp.eye(heads, dtype=w.dtype)
    return jnp.einsum('hij,hg->higj', w, eye).reshape(heads * blk, heads * blk)


def _rglru(xc, gc, conv_w, conv_b, w_r, b_r, w_i, b_i, lam, batch, seq):
    n, width = xc.shape
    ts = min(LRU_CHUNK, seq)
    nc = seq // ts
    wri = jnp.concatenate([_block_diag(w_r), _block_diag(w_i)], axis=1).astype(BF16)
    bri = jnp.concatenate([b_r, b_i]).astype(F32).reshape(1, 2 * width)
    blk = lambda b, j: (b * nc + j, 0)
    return pl.pallas_call(
        _lru_kernel,
        grid=(batch, nc),
        in_specs=[pl.BlockSpec((ts, width), blk), pl.BlockSpec((ts, width), blk),
                  _resident((CONV_W, width)), _resident((1, width)),
                  _resident((width, 2 * width)), _resident((1, 2 * width)),
                  _resident((1, width))],
        out_specs=pl.BlockSpec((ts, width), blk),
        out_shape=jax.ShapeDtypeStruct((n, width), BF16),
        scratch_shapes=[pltpu.VMEM((SUBLANES, width), F32), pltpu.VMEM((1, width), F32)],
        compiler_params=_cparams("parallel", "arbitrary"),
        name="rglru",
    )(xc, gc, conv_w.astype(F32), conv_b.astype(F32).reshape(1, width), wri, bri,
      lam.astype(F32).reshape(1, width))


def _merge_kernel(x_ref, oa_ref, ob_ref, oc_ref, gt_ref, bg_ref, wa_ref, wb_ref, wc_ref, wo_ref, o_ref):
    tm, d = x_ref.shape
    branches = ((oa_ref, wa_ref), (ob_ref, wb_ref), (oc_ref, wc_ref))
    halves = [slice(r * (tm // 2), (r + 1) * (tm // 2)) for r in range(2)]
    proj = [[_dot(b_ref[rows, :], w_ref[...]) for b_ref, w_ref in branches] for rows in halves]
    for rows, terms in zip(halves, proj):
        merged = None
        for idx, term in enumerate(terms):
            cols = slice(idx * d, (idx + 1) * d)
            gate = jax.nn.sigmoid(gt_ref[rows, cols].astype(F32) + bg_ref[:, cols])
            merged = gate * term if merged is None else merged + gate * term
        o_ref[rows, :] = x_ref[rows, :] + _dot(merged.astype(BF16), wo_ref[...])


def _merge(x2, o_a, o_b, o_c, gates, b_gate, wa, wb, wc, wo):
    n, d = x2.shape
    tm = min(ROW_TILE, n)
    row = lambda i: (i, 0)
    return pl.pallas_call(
        _merge_kernel,
        grid=(n // tm,),
        in_specs=[pl.BlockSpec((tm, d), row),
                  pl.BlockSpec((tm, o_a.shape[1]), row), pl.BlockSpec((tm, o_b.shape[1]), row),
                  pl.BlockSpec((tm, o_c.shape[1]), row), pl.BlockSpec((tm, gates.shape[1]), row),
                  _resident((1, gates.shape[1])),
                  _resident(wa.shape), _resident(wb.shape), _resident(wc.shape), _resident(wo.shape)],
        out_specs=pl.BlockSpec((tm, d), row),
        out_shape=jax.ShapeDtypeStruct((n, d), F32),
        compiler_params=_cparams("parallel"),
        name="merge",
    )(x2, o_a, o_b, o_c, gates, b_gate.astype(F32).reshape(1, -1), wa, wb, wc, wo)


def _swiglu_tile(h, wg_ref, wu_ref, wd_ref, ff_chunk):
    d_ff = wg_ref.shape[-1]
    starts = list(range(0, d_ff, ff_chunk))

    def gate_up(c):
        return _dot(h, wg_ref[:, c:c + ff_chunk]), _dot(h, wu_ref[:, c:c + ff_chunk])

    acc = None
    pending = gate_up(starts[0])
    for i, c in enumerate(starts):
        g, u = pending
        if i + 1 < len(starts):
            pending = gate_up(starts[i + 1])
        act = (g * jax.nn.sigmoid(g) * u).astype(BF16)
        part = _dot(act, wd_ref[c:c + ff_chunk, :])
        acc = part if acc is None else acc + part
    return acc


def _ffn_chunk(d_ff):
    for c in (512, 896, 256, 128):
        if d_ff % c == 0:
            return c
    return d_ff


def _dense_ffn_kernel(ff_chunk, x_ref, g_ref, wg_ref, wu_ref, wd_ref, o_ref):
    x = x_ref[...]
    h = _rms(x, g_ref[...]).astype(BF16)
    o_ref[...] = x + _swiglu_tile(h, wg_ref, wu_ref, wd_ref, ff_chunk)


def _dense_ffn(x2, gain, wg, wu, wd):
    n, d = x2.shape
    tm = min(ROW_TILE, n)
    return pl.pallas_call(
        functools.partial(_dense_ffn_kernel, _ffn_chunk(wg.shape[1])),
        grid=(n // tm,),
        in_specs=[pl.BlockSpec((tm, d), lambda i: (i, 0)), _resident((1, d)),
                  _resident(wg.shape), _resident(wu.shape), _resident(wd.shape)],
        out_specs=pl.BlockSpec((tm, d), lambda i: (i, 0)),
        out_shape=jax.ShapeDtypeStruct((n, d), F32),
        compiler_params=_cparams("parallel"),
        name="dense_ffn",
    )(x2, gain, wg, wu, wd)


def _router_kernel(n_experts, stride, x_ref, g_ref, wr_ref, tril_ref, h_ref, idx_ref, wt_ref, total_ref, count_ref):
    h = _rms(x_ref[...], g_ref[...])
    h_ref[...] = h.astype(h_ref.dtype)
    w = wr_ref[...]
    h1 = h.astype(BF16)
    h2 = (h - h1.astype(F32)).astype(BF16)
    h3 = (h - h1.astype(F32) - h2.astype(F32)).astype(BF16)
    a, b, c = _dot(h1, w), _dot(h2, w), _dot(h3, w)

    def term(x, k):
        return x if k == 0 else pltpu.roll(x, LANES - k * stride, 1)

    logits = ((term(a, 0) + (term(a, 1) + term(b, 0)))
              + (term(b, 1) + term(a, 2) + term(c, 0)))
    lane = lax.broadcasted_iota(jnp.int32, logits.shape, 1)
    logits = jnp.where(lane < n_experts, logits, NEG_BIG)
    m1 = jnp.max(logits, axis=-1, keepdims=True)
    i1 = jnp.min(jnp.where(logits == m1, lane, LANES), axis=-1, keepdims=True)
    rest = jnp.where(lane == i1, NEG_BIG, logits)
    m2 = jnp.max(rest, axis=-1, keepdims=True)
    i2 = jnp.min(jnp.where(rest == m2, lane, LANES), axis=-1, keepdims=True)
    e2 = jnp.exp(m2 - m1)
    wt1 = 1.0 / (1.0 + e2)
    wt2 = e2 / (1.0 + e2)
    wt_ref[...] = jnp.where(lane == 0, wt1, jnp.where(lane == 1, wt2, 0.0))

    @pl.when(pl.program_id(0) == 0)
    def _():
        count_ref[...] = jnp.zeros_like(count_ref)

    chosen = jnp.where((lane == i1) | (lane == i2), 1.0, 0.0)
    before = _dot(tril_ref[...], chosen.astype(BF16)) + count_ref[...]
    r1 = jnp.sum(jnp.where(lane == i1, before, 0.0), axis=-1, keepdims=True).astype(jnp.int32)
    r2 = jnp.sum(jnp.where(lane == i2, before, 0.0), axis=-1, keepdims=True).astype(jnp.int32)
    count_ref[...] = count_ref[...] + jnp.sum(chosen, axis=0, keepdims=True)
    total_ref[...] = count_ref[...]
    route = jnp.where(lane == 0, i1, jnp.where(lane == 1, i2,
                      jnp.where(lane == 2, r1, jnp.where(lane == 3, r2, 0))))
    idx_ref[...] = jnp.transpose(route)[:idx_ref.shape[0], :]


def _router(x2, gain, w_router):
    n, d = x2.shape
    n_experts = w_router.shape[1]
    tm = min(ROW_TILE, n)
    stride = SUBLANES * (-(-n_experts // SUBLANES))
    assert 3 * stride <= LANES
    w = w_router.astype(F32)
    w1 = w.astype(BF16)
    w2 = (w - w1.astype(F32)).astype(BF16)
    w3 = (w - w1.astype(F32) - w2.astype(F32)).astype(BF16)
    wr = jnp.zeros((d, LANES), BF16)
    for k, wk in enumerate((w1, w2, w3)):
        wr = wr.at[:, k * stride:k * stride + n_experts].set(wk)
    tril = jnp.asarray(np.tril(np.ones((tm, tm), np.float32), -1), BF16)
    row = lambda i: (i, 0)
    return pl.pallas_call(
        functools.partial(_router_kernel, n_experts, stride),
        grid=(n // tm,),
        in_specs=[pl.BlockSpec((tm, d), row), _resident((1, d)), _resident((d, LANES)), _resident((tm, tm))],
        out_specs=[pl.BlockSpec((tm, d), row), pl.BlockSpec((SUBLANES, tm), lambda i: (0, i)),
                   pl.BlockSpec((tm, LANES), row), pl.BlockSpec((1, LANES), lambda i: (0, 0))],
        out_shape=[jax.ShapeDtypeStruct((n, d), BF16), jax.ShapeDtypeStruct((SUBLANES, n), jnp.int32),
                   jax.ShapeDtypeStruct((n, LANES), F32), jax.ShapeDtypeStruct((1, LANES), F32)],
        scratch_shapes=[pltpu.VMEM((1, LANES), F32)],
        compiler_params=_cparams("arbitrary"),
        name="router",
    )(x2, gain, wr, tril)


def _moe_kernel(ff_chunk, te_ref, tv_ref, x_ref, wg_ref, wu_ref, wd_ref, o_ref):
    t = pl.program_id(0)

    @pl.when(tv_ref[t] > 0)
    def _():
        y = _swiglu_tile(x_ref[...], wg_ref.at[0], wu_ref.at[0], wd_ref.at[0], ff_chunk)
        o_ref[...] = y.astype(o_ref.dtype)

    @pl.when(tv_ref[t] == 0)
    def _():
        o_ref[...] = jnp.zeros_like(o_ref)


def _moe_experts(xs, tile_expert, tile_valid, wg, wu, wd):
    p, d = xs.shape
    d_ff = wg.shape[2]
    tm = MOE_TILE
    grid_spec = pltpu.PrefetchScalarGridSpec(
        num_scalar_prefetch=2,
        grid=(p // tm,),
        in_specs=[pl.BlockSpec((tm, d), lambda t, te, tv: (t, 0)),
                  pl.BlockSpec((1, d, d_ff), lambda t, te, tv: (te[t], 0, 0)),
                  pl.BlockSpec((1, d, d_ff), lambda t, te, tv: (te[t], 0, 0)),
                  pl.BlockSpec((1, d_ff, d), lambda t, te, tv: (te[t], 0, 0))],
        out_specs=pl.BlockSpec((tm, d), lambda t, te, tv: (t, 0)),
    )
    return pl.pallas_call(
        functools.partial(_moe_kernel, _ffn_chunk(d_ff)),
        grid_spec=grid_spec,
        out_shape=jax.ShapeDtypeStruct((p, d), BF16),
        compiler_params=_cparams("arbitrary"),
        name="moe_experts",
    )(tile_expert, tile_valid, xs, wg, wu, wd)


def _combine_kernel(x_ref, wt_ref, y0_ref, y1_ref, o_ref):
    wt = wt_ref[...]
    o_ref[...] = x_ref[...] + (wt[:, 0:1] * y0_ref[...].astype(F32) + wt[:, 1:2] * y1_ref[...].astype(F32))


def _combine(x2, wt, y0, y1):
    n, d = x2.shape
    tm = min(ROW_TILE, n)
    row = lambda i: (i, 0)
    return pl.pallas_call(
        _combine_kernel,
        grid=(n // tm,),
        in_specs=[pl.BlockSpec((tm, d), row), pl.BlockSpec((tm, LANES), row),
                  pl.BlockSpec((tm, d), row), pl.BlockSpec((tm, d), row)],
        out_specs=pl.BlockSpec((tm, d), row),
        out_shape=jax.ShapeDtypeStruct((n, d), F32),
        compiler_params=_cparams("parallel"),
        name="moe_combine",
    )(x2, wt, y0, y1)


def _moe_ffn(x2, gain, w_router, expert_weights, layer):
    n, d = x2.shape
    n_experts = w_router.shape[1]
    tm = MOE_TILE
    h, route, top_w, totals = _router(x2, gain, w_router)

    sizes = totals[0, :n_experts].astype(jnp.int32)
    padded = ((sizes + tm - 1) // tm) * tm
    starts = jnp.cumsum(padded) - padded
    pos = []
    for k in range(TOP_K):
        expert, rank = route[k], route[TOP_K + k]
        base = jnp.zeros_like(rank)
        for e in range(n_experts):
            base = base + jnp.where(expert == e, starts[e], 0)
        pos.append(rank + base)
    n_tiles = (n * TOP_K) // tm + n_experts
    p = n_tiles * tm
    token = jnp.tile(jnp.arange(n, dtype=jnp.int32), TOP_K)
    row_src = jnp.zeros((p,), jnp.int32).at[jnp.concatenate(pos)].set(
        token, unique_indices=True, mode="promise_in_bounds")
    tile_start = jnp.arange(n_tiles, dtype=jnp.int32) * tm
    ends = starts + padded
    tile_expert = jnp.minimum(jnp.sum((tile_start[:, None] >= ends[None, :]).astype(jnp.int32), axis=1),
                              n_experts - 1).astype(jnp.int32)
    tile_valid = (tile_start < ends[-1]).astype(jnp.int32)

    xs = h.at[row_src].get(mode="promise_in_bounds")
    wg, wu, wd = (_to_bf16(w, layer, after=row_src) for w in expert_weights)
    ys = _moe_experts(xs, tile_expert, tile_valid, wg, wu, wd)
    y0 = ys.at[pos[0]].get(mode="promise_in_bounds")
    y1 = ys.at[pos[1]].get(mode="promise_in_bounds")
    return _combine(x2, top_w, y0, y1)


def kernel(x, attn_norm, w_in, b_gate, q_norm, k_norm, sinks, conv_w, conv_b, lru_w_r, lru_b_r, lru_w_i, lru_b_i, lru_lambda, w_proj_a, w_proj_b, w_proj_c, w_out, ffn_norm, w_ffn_gate, w_ffn_up, w_ffn_down, w_router, w_exp_gate, w_exp_up, w_exp_down):
    batch, seq, d = x.shape
    depth = w_in.shape[0]
    swa_q = w_proj_a.shape[1]
    sb_w = w_proj_b.shape[1]
    lru_w = w_proj_c.shape[1]
    swa_kv = SWA_KV_HEADS * HEAD_DIM
    widths = (swa_q, swa_kv, swa_kv, sb_w, sb_w, sb_w, lru_w, lru_w, N_BRANCHES * d)
    assert sum(widths) == w_in.shape[2]
    assert seq % SB_QBLK == 0 and seq % WINDOW == 0

    x2 = x.reshape(batch * seq, d).astype(F32)
    for l in range(depth):
        gain = attn_norm[l].astype(F32).reshape(1, d)
        qa, ka, va, qb, kb, vb, xc, gc, gates = _in_proj(x2, gain, _to_bf16(w_in, l), widths,
                                                         q_norm[l], k_norm[l])
        o_a = _swa(qa, ka, va, sinks[l], batch, seq)
        o_b = _stick_breaking(qb, kb, vb, batch, seq)
        o_c = _rglru(xc, gc, conv_w[l], conv_b[l], lru_w_r[l], lru_b_r[l], lru_w_i[l], lru_b_i[l],
                     lru_lambda[l], batch, seq)
        x2 = _merge(x2, o_a, o_b, o_c, gates, b_gate[l],
                    _to_bf16(w_proj_a, l), _to_bf16(w_proj_b, l), _to_bf16(w_proj_c, l),
                    _to_bf16(w_out, l))
        fgain = ffn_norm[l].astype(F32).reshape(1, d)
        j = l // 2
        if l % 2 == 0:
            x2 = _dense_ffn(x2, fgain, _to_bf16(w_ffn_gate, j), _to_bf16(w_ffn_up, j),
                            _to_bf16(w_ffn_down, j))
        else:
            x2 = _moe_ffn(x2, fgain, w_router[j], (w_exp_gate, w_exp_up, w_exp_down), j)
    return x2.reshape(batch, seq, d).astype(x.dtype)
```

```python
import functools

import numpy as np
import jax
import jax.numpy as jnp
from jax import lax
from jax.experimental import pallas as pl
from jax.experimental.pallas import tpu as pltpu

F32 = jnp.float32
BF16 = jnp.bfloat16

HEAD_DIM = 64
SWA_KV_HEADS = 2
WINDOW = 128
LRU_HEADS = 8
LRU_C = 8.0
CONV_W = 4
N_BRANCHES = 3
TOP_K = 2
EPS = 1e-6

LANES = 128
SUBLANES = 8
VMEM_LIMIT = 56 * 1024 * 1024
NEG_BIG = -1e30

SWA_BLOCKS_PER_STEP = 4
SB_QBLK = 256
SB_PAIRS_PER_STEP = 4
SB_UNDERFLOW = 105.0
LRU_CHUNK = 512
ROW_TILE = 512
MOE_TILE = 512


def _cparams(*sem):
    return pltpu.CompilerParams(dimension_semantics=sem, vmem_limit_bytes=VMEM_LIMIT)


def _resident(shape):
    nd = len(shape)
    return pl.BlockSpec(shape, lambda *_: (0,) * nd, pipeline_mode=pl.Buffered(1))


def _dot(a, b):
    return jnp.dot(a, b, preferred_element_type=F32)


def _dot_nt(a, b):
    return lax.dot_general(a, b, (((1,), (1,)), ((), ())), preferred_element_type=F32)


def _split_dot(a, b_bf16):
    hi = a.astype(BF16)
    lo = (a - hi.astype(F32)).astype(BF16)
    return _dot(hi, b_bf16) + _dot(lo, b_bf16)


def _rms(x, gain):
    ms = jnp.mean(x * x, axis=-1, keepdims=True)
    return x * lax.rsqrt(ms + EPS) * gain


CAST_BLOCK_BYTES = 4 * 1024 * 1024


def _cast_kernel(x_ref, o_ref):
    o_ref[...] = x_ref[...].astype(o_ref.dtype)


def _to_bf16(w, layer):
    shape = w.shape[1:]
    cols = shape[-1]
    rows = int(np.prod(shape[:-1]))
    tr = rows
    while tr * cols * 4 > CAST_BLOCK_BYTES and tr % 32 == 0:
        tr //= 2
    out = pl.pallas_call(
        _cast_kernel,
        grid=(rows // tr,),
        in_specs=[pl.BlockSpec((None, tr, cols), lambda i: (layer, i, 0))],
        out_specs=pl.BlockSpec((tr, cols), lambda i: (i, 0)),
        out_shape=jax.ShapeDtypeStruct((rows, cols), BF16),
        compiler_params=_cparams("parallel"),
        name="cast_bf16",
    )(w.reshape(w.shape[0], rows, cols))
    return out.reshape(shape)


def _head_norm(xf, bd, gain):
    ms = _dot((xf * xf).astype(BF16), bd)
    return xf * lax.rsqrt(ms + EPS) * gain


def _in_proj_kernel(widths, x_ref, g_ref, w_ref, bdq_ref, bdk_ref, qg_ref, kg_ref, *out_refs):
    h = _rms(x_ref[...], g_ref[...]).astype(BF16)
    normed = ((bdq_ref, qg_ref), (bdk_ref, kg_ref))
    offs = [sum(widths[:i]) for i in range(len(widths))]
    accs = [_dot(h, w_ref[:, offs[i]:offs[i] + widths[i]]) for i in range(len(normed))]
    for idx in range(len(normed), len(widths)):
        ref, width, off = out_refs[idx], widths[idx], offs[idx]
        for c in range(0, width, 1024):
            cw = min(1024, width - c)
            ref[:, c:c + cw] = _dot(h, w_ref[:, off + c:off + c + cw]).astype(ref.dtype)
    for idx, (bd_ref, gain_ref) in enumerate(normed):
        out_refs[idx][...] = _head_norm(accs[idx], bd_ref[...], gain_ref[...]).astype(out_refs[idx].dtype)


def _block_diag_mean(width):
    idx = np.arange(width) // HEAD_DIM
    return jnp.asarray((idx[:, None] == idx[None, :]).astype(np.float32) / HEAD_DIM, BF16)


def _in_proj(x2, gain, w_bf16, widths, q_norm, k_norm):
    n, d = x2.shape
    tm = min(ROW_TILE, n)
    qw, kw = widths[0], widths[1]
    qg = (jnp.tile(q_norm.astype(F32), qw // HEAD_DIM) * (HEAD_DIM ** -0.5)).reshape(1, qw)
    kg = jnp.tile(k_norm.astype(F32), kw // HEAD_DIM).reshape(1, kw)
    return pl.pallas_call(
        functools.partial(_in_proj_kernel, widths),
        grid=(n // tm,),
        in_specs=[pl.BlockSpec((tm, d), lambda i: (i, 0)),
                  _resident((1, d)),
                  _resident(w_bf16.shape),
                  _resident((qw, qw)), _resident((kw, kw)),
                  _resident((1, qw)), _resident((1, kw))],
        out_specs=[pl.BlockSpec((tm, w), lambda i: (i, 0)) for w in widths],
        out_shape=[jax.ShapeDtypeStruct((n, w), BF16) for w in widths],
        compiler_params=_cparams("parallel"),
        name="in_proj",
    )(x2, gain, w_bf16, _block_diag_mean(qw), _block_diag_mean(kw), qg, kg)


def _swa_kernel(n_heads, blk, q_ref, kc_ref, kp_ref, vc_ref, vp_ref, sink_ref, o_ref):
    n = pl.program_id(1)
    n_blocks = q_ref.shape[0] // blk
    group = n_heads // SWA_KV_HEADS

    upper = (lax.broadcasted_iota(jnp.int32, (blk, blk), 1)
             > lax.broadcasted_iota(jnp.int32, (blk, blk), 0))
    no_prev = upper & (n == 0)
    lane = lax.broadcasted_iota(jnp.int32, (blk, LANES), 1)
    low = lane < HEAD_DIM

    def swapped(t):
        return (t, pltpu.roll(t, HEAD_DIM, 1))

    rows = [slice(b * blk, (b + 1) * blk) for b in range(n_blocks)]
    k_blocks = [swapped(kp_ref[...])] + [swapped(kc_ref[r, :]) for r in rows]
    v_blocks = [swapped(vp_ref[...])] + [swapped(vc_ref[r, :]) for r in rows]
    kt_blocks = [tuple(jnp.transpose(t) for t in pair) for pair in k_blocks]

    jobs = [(b, j) for b in range(n_blocks) for j in range(n_heads)]
    sel = [(j // group + j % 2) % 2 for j in range(n_heads)]
    s_prev, s_cur, p_prev, p_cur, denom, outs = {}, {}, {}, {}, {}, {}

    def scores(t):
        b, j = jobs[t]
        q_pair = q_ref[rows[b], (j // 2) * LANES:(j // 2 + 1) * LANES]
        qm = jnp.where(low if j % 2 == 0 else ~low, q_pair, jnp.zeros_like(q_pair))
        s_prev[t] = _dot(qm, kt_blocks[b][sel[j]])
        s_cur[t] = _dot(qm, kt_blocks[b + 1][sel[j]])

    def softmax(t):
        b, j = jobs[t]
        s = jnp.where(upper, s_prev[t], s_cur[t])
        if b == 0:
            s = jnp.where(no_prev, NEG_BIG, s)
        sink = sink_ref[j]
        m = jnp.maximum(jnp.max(s, axis=-1, keepdims=True), sink)
        p = jnp.exp(s - m)
        denom[t] = jnp.sum(p, axis=-1, keepdims=True) + jnp.exp(sink - m)
        p_prev[t] = jnp.where(upper, p, 0.0).astype(BF16)
        p_cur[t] = jnp.where(upper, 0.0, p).astype(BF16)

    def values(t):
        b, j = jobs[t]
        outs[t] = (_dot(p_prev[t], v_blocks[b][sel[j]]), _dot(p_cur[t], v_blocks[b + 1][sel[j]]))

    def finish(t):
        b, j = jobs[t]
        outs[t] = (outs[t][0] + outs[t][1]) / denom[t]
        if j % 2 == 1:
            o_ref[rows[b], (j // 2) * LANES:(j // 2 + 1) * LANES] = jnp.where(
                low, outs[t - 1], outs[t]).astype(o_ref.dtype)

    for stage in (scores, softmax, values, finish):
        for t in range(len(jobs)):
            stage(t)


def _swa(qa, ka, va, sinks, batch, seq):
    n, qw = qa.shape
    kw = ka.shape[1]
    n_heads = qw // HEAD_DIM
    blk = WINDOW
    per_step = min(SWA_BLOCKS_PER_STEP, seq // blk)
    assert seq % (per_step * blk) == 0
    rows = per_step * blk
    ns = seq // rows
    cur = lambda b, i: (b * ns + i, 0)
    prev = lambda b, i: (b * ns * per_step + jnp.maximum(i * per_step - 1, 0), 0)
    return pl.pallas_call(
        functools.partial(_swa_kernel, n_heads, blk),
        grid=(batch, ns),
        in_specs=[pl.BlockSpec((rows, qw), cur),
                  pl.BlockSpec((rows, kw), cur), pl.BlockSpec((blk, kw), prev),
                  pl.BlockSpec((rows, kw), cur), pl.BlockSpec((blk, kw), prev),
                  pl.BlockSpec(memory_space=pltpu.SMEM)],
        out_specs=pl.BlockSpec((rows, qw), cur),
        out_shape=jax.ShapeDtypeStruct((n, qw), BF16),
        compiler_params=_cparams("parallel", "parallel"),
        name="swa",
    )(qa, ka, ka, va, va, sinks.astype(F32))


def _sb_kernel(q_ref, k_ref, v_ref, tri_ref, o_ref):
    i = pl.program_id(2)
    qb, width = q_ref.shape
    n_heads = 2 * (width // LANES)
    lane = lax.broadcasted_iota(jnp.int32, (qb, LANES), 1)
    low = lane < HEAD_DIM
    scale = jnp.asarray(HEAD_DIM ** -0.5, q_ref.dtype)
    q_heads = []
    for h in range(n_heads):
        q_pair = q_ref[:, (h // 2) * LANES:(h // 2 + 1) * LANES]
        q_heads.append(jnp.where(low if h % 2 == 0 else ~low, q_pair, jnp.zeros_like(q_pair)) * scale)
    tri = tri_ref[...]
    strict = (lax.broadcasted_iota(jnp.int32, (qb, qb), 1)
              < lax.broadcasted_iota(jnp.int32, (qb, qb), 0))

    def step(chunks, carries, accs):
        carries, accs = list(carries), list(accs)
        jobs = [(ci, h) for ci in range(len(chunks)) for h in range(n_heads)]
        z, log_beta, fail, fail_sum, later, w = {}, {}, {}, {}, {}, {}

        def cols(ref, ci, h):
            start = pl.multiple_of(chunks[ci][0] * qb, qb)
            return ref[pl.ds(start, qb), (h // 2) * LANES:(h // 2 + 1) * LANES]

        def scores(j):
            ci, h = jobs[j]
            z[j] = _dot_nt(q_heads[h], cols(k_ref, ci, h))

        def softplus(j):
            sp = jnp.maximum(z[j], 0.0) + jnp.log(1.0 + jnp.exp(-jnp.abs(z[j])))
            log_beta[j] = z[j] - sp
            spm = jnp.where(strict, sp, 0.0) if chunks[jobs[j][0]][1] else sp
            fail[j] = spm.astype(BF16)
            fail_sum[j] = jnp.sum(spm, axis=-1, keepdims=True)

        def suffix_sums(j):
            later[j] = _dot(fail[j], tri)

        def weights(j):
            ci, h = jobs[j]
            wj = jnp.exp(log_beta[j] - (later[j] + carries[h]))
            w[j] = (jnp.where(strict, wj, 0.0) if chunks[ci][1] else wj).astype(BF16)
            carries[h] = carries[h] + fail_sum[j]

        def values(j):
            ci, h = jobs[j]
            accs[h] = accs[h] + _dot(w[j], cols(v_ref, ci, h))

        for stage in (scores, softplus, suffix_sums, weights, values):
            for j in range(len(jobs)):
                stage(j)
        return tuple(carries), tuple(accs)

    carries = (jnp.zeros((qb, 1), F32),) * n_heads
    accs = (jnp.zeros((qb, LANES), F32),) * n_heads
    carries, accs = lax.cond(i == 0,
                             lambda: step([(i, True)], carries, accs),
                             lambda: step([(i, True), (i - 1, False)], carries, accs))

    def cond(state):
        t, carries, _ = state
        lowest = functools.reduce(jnp.minimum, [jnp.min(c) for c in carries])
        return (t < i - 1) & (lowest <= SB_UNDERFLOW)

    def body(state):
        t, carries, accs = state
        carries, accs = step([(i - 2 - t, False)], carries, accs)
        return t + 1, carries, accs

    _, _, accs = lax.while_loop(cond, body, (jnp.int32(0), carries, accs))
    for p in range(width // LANES):
        o_ref[:, p * LANES:(p + 1) * LANES] = jnp.where(low, accs[2 * p], accs[2 * p + 1]).astype(o_ref.dtype)


def _stick_breaking(qb, kb, vb, batch, seq):
    n, w = qb.shape
    blk = min(SB_QBLK, seq)
    nq = seq // blk
    cols = SB_PAIRS_PER_STEP * LANES
    tri = jnp.asarray(np.tril(np.ones((blk, blk), np.float32), -1), BF16)
    return pl.pallas_call(
        _sb_kernel,
        grid=(batch, w // cols, nq),
        in_specs=[pl.BlockSpec((blk, cols), lambda b, p, i: (b * nq + i, p)),
                  pl.BlockSpec((seq, cols), lambda b, p, i: (b, p)),
                  pl.BlockSpec((seq, cols), lambda b, p, i: (b, p)),
                  _resident((blk, blk))],
        out_specs=pl.BlockSpec((blk, cols), lambda b, p, i: (b * nq + i, p)),
        out_shape=jax.ShapeDtypeStruct((n, w), BF16),
        compiler_params=_cparams("parallel", "parallel", "arbitrary"),
        name="stick_breaking",
    )(qb, kb, vb, tri)


def _gelu_tanh(x):
    return 0.5 * x * (1.0 + jnp.tanh(np.sqrt(2.0 / np.pi) * (x + 0.044715 * (x * x * x))))


def _lru_kernel(x_ref, g_ref, cw_ref, cb_ref, wri_ref, bri_ref, lam_ref, o_ref, tail_ref, h_ref):
    j = pl.program_id(1)
    ts, width = x_ref.shape

    @pl.when(j == 0)
    def _():
        tail_ref[...] = jnp.zeros_like(tail_ref)
        h_ref[...] = jnp.zeros_like(h_ref)

    x = x_ref[...].astype(F32)
    tail = tail_ref[...]
    row8 = lax.broadcasted_iota(jnp.int32, (SUBLANES, width), 0)
    y = x * cw_ref[CONV_W - 1:CONV_W, :] + cb_ref[...]
    for d in range(1, CONV_W):
        xs = pltpu.roll(x, d, 0)
        top = jnp.where(row8 < d, pltpu.roll(tail, d, 0), xs[:SUBLANES])
        xs = jnp.concatenate([top, xs[SUBLANES:]], axis=0)
        y = y + xs * cw_ref[CONV_W - 1 - d:CONV_W - d, :]
    tail_ref[...] = x[ts - SUBLANES:]

    ri = _dot(y.astype(BF16), wri_ref[...]) + bri_ref[...]
    r = jax.nn.sigmoid(ri[:, :width])
    gate_i = jax.nn.sigmoid(ri[:, width:])
    lam = lam_ref[...]
    softplus_neg_lam = jnp.maximum(-lam, 0.0) + jnp.log(1.0 + jnp.exp(-jnp.abs(lam)))
    a = jnp.exp2(r * ((-LRU_C * np.log2(np.e)) * softplus_neg_lam))
    u = jnp.sqrt(1.0 - a * a) * (gate_i * y)

    grouped = (ts // SUBLANES, SUBLANES, width)
    a, u = a.reshape(grouped), u.reshape(grouped)
    in_group = lax.broadcasted_iota(jnp.int32, grouped, 1)
    d = 1
    while d < SUBLANES:
        keep = in_group >= d
        a_sh = jnp.where(keep, pltpu.roll(a, d, 1), 1.0)
        u_sh = jnp.where(keep, pltpu.roll(u, d, 1), 0.0)
        u = a * u_sh + u
        a = a * a_sh
        d *= 2
    a, u = a.reshape(ts, width), u.reshape(ts, width)
    state = h_ref[...]
    groups = []
    for g in range(ts // SUBLANES):
        rows = slice(g * SUBLANES, (g + 1) * SUBLANES)
        hg = u[rows] + a[rows] * state
        groups.append(hg)
        state = hg[SUBLANES - 1:SUBLANES, :]
    h = jnp.concatenate(groups, axis=0)
    h_ref[...] = state
    o_ref[...] = (h * _gelu_tanh(g_ref[...].astype(F32))).astype(o_ref.dtype)


def _block_diag(w):
    heads, blk, _ = w.shape
    eye = jnp.eye(heads, dtype=w.dtype)
    return jnp.einsum('hij,hg->higj', w, eye).reshape(heads * blk, heads * blk)


def _rglru(xc, gc, conv_w, conv_b, w_r, b_r, w_i, b_i, lam, batch, seq):
    n, width = xc.shape
    ts = min(LRU_CHUNK, seq)
    nc = seq // ts
    wri = jnp.concatenate([_block_diag(w_r), _block_diag(w_i)], axis=1).astype(BF16)
    bri = jnp.concatenate([b_r, b_i]).astype(F32).reshape(1, 2 * width)
    blk = lambda b, j: (b * nc + j, 0)
    return pl.pallas_call(
        _lru_kernel,
        grid=(batch, nc),
        in_specs=[pl.BlockSpec((ts, width), blk), pl.BlockSpec((ts, width), blk),
                  _resident((CONV_W, width)), _resident((1, width)),
                  _resident((width, 2 * width)), _resident((1, 2 * width)),
                  _resident((1, width))],
        out_specs=pl.BlockSpec((ts, width), blk),
        out_shape=jax.ShapeDtypeStruct((n, width), BF16),
        scratch_shapes=[pltpu.VMEM((SUBLANES, width), F32), pltpu.VMEM((1, width), F32)],
        compiler_params=_cparams("parallel", "arbitrary"),
        name="rglru",
    )(xc, gc, conv_w.astype(F32), conv_b.astype(F32).reshape(1, width), wri, bri,
      lam.astype(F32).reshape(1, width))


def _merge_kernel(x_ref, oa_ref, ob_ref, oc_ref, gt_ref, bg_ref, wa_ref, wb_ref, wc_ref, wo_ref, o_ref):
    tm, d = x_ref.shape
    branches = ((oa_ref, wa_ref), (ob_ref, wb_ref), (oc_ref, wc_ref))
    halves = [slice(r * (tm // 2), (r + 1) * (tm // 2)) for r in range(2)]
    proj = [[_dot(b_ref[rows, :], w_ref[...]) for b_ref, w_ref in branches] for rows in halves]
    for rows, terms in zip(halves, proj):
        merged = None
        for idx, term in enumerate(terms):
            cols = slice(idx * d, (idx + 1) * d)
            gate = jax.nn.sigmoid(gt_ref[rows, cols].astype(F32) + bg_ref[:, cols])
            merged = gate * term if merged is None else merged + gate * term
        o_ref[rows, :] = x_ref[rows, :] + _dot(merged.astype(BF16), wo_ref[...])


def _merge(x2, o_a, o_b, o_c, gates, b_gate, wa, wb, wc, wo):
    n, d = x2.shape
    tm = min(ROW_TILE, n)
    row = lambda i: (i, 0)
    return pl.pallas_call(
        _merge_kernel,
        grid=(n // tm,),
        in_specs=[pl.BlockSpec((tm, d), row),
                  pl.BlockSpec((tm, o_a.shape[1]), row), pl.BlockSpec((tm, o_b.shape[1]), row),
                  pl.BlockSpec((tm, o_c.shape[1]), row), pl.BlockSpec((tm, gates.shape[1]), row),
                  _resident((1, gates.shape[1])),
                  _resident(wa.shape), _resident(wb.shape), _resident(wc.shape), _resident(wo.shape)],
        out_specs=pl.BlockSpec((tm, d), row),
        out_shape=jax.ShapeDtypeStruct((n, d), F32),
        compiler_params=_cparams("parallel"),
        name="merge",
    )(x2, o_a, o_b, o_c, gates, b_gate.astype(F32).reshape(1, -1), wa, wb, wc, wo)


def _swiglu_tile(h, wg_ref, wu_ref, wd_ref, ff_chunk):
    d_ff = wg_ref.shape[-1]
    starts = list(range(0, d_ff, ff_chunk))

    def gate_up(c):
        return _dot(h, wg_ref[:, c:c + ff_chunk]), _dot(h, wu_ref[:, c:c + ff_chunk])

    acc = None
    pending = gate_up(starts[0])
    for i, c in enumerate(starts):
        g, u = pending
        if i + 1 < len(starts):
            pending = gate_up(starts[i + 1])
        act = (g * jax.nn.sigmoid(g) * u).astype(BF16)
        part = _dot(act, wd_ref[c:c + ff_chunk, :])
        acc = part if acc is None else acc + part
    return acc


def _ffn_chunk(d_ff):
    for c in (512, 896, 256, 128):
        if d_ff % c == 0:
            return c
    return d_ff


def _dense_ffn_kernel(ff_chunk, n_side, x_ref, g_ref, wg_ref, wu_ref, wd_ref, *refs):
    side_in, o_ref, side_out = refs[:n_side], refs[n_side], refs[n_side + 1:]
    x = x_ref[...]
    h = _rms(x, g_ref[...]).astype(BF16)
    o_ref[...] = x + _swiglu_tile(h, wg_ref, wu_ref, wd_ref, ff_chunk)
    for src, dst in zip(side_in, side_out):
        dst[...] = src[...].astype(dst.dtype)


def _dense_ffn(x2, gain, wg, wu, wd, side_casts=()):
    n, d = x2.shape
    tm = min(ROW_TILE, n)
    steps = n // tm
    side_in, side_specs_in, side_specs_out, side_shapes, side_final = [], [], [], [], []
    for w, layer in side_casts:
        shape = w.shape[1:]
        cols = shape[-1]
        rows = int(np.prod(shape[:-1]))
        assert rows % (steps * 2 * SUBLANES) == 0
        tr = rows // steps
        side_in.append(w.reshape(w.shape[0], rows, cols))
        side_specs_in.append(pl.BlockSpec((None, tr, cols), lambda i, layer=layer: (layer, i, 0)))
        side_specs_out.append(pl.BlockSpec((tr, cols), lambda i: (i, 0)))
        side_shapes.append(jax.ShapeDtypeStruct((rows, cols), BF16))
        side_final.append(shape)
    outs = pl.pallas_call(
        functools.partial(_dense_ffn_kernel, _ffn_chunk(wg.shape[1]), len(side_casts)),
        grid=(steps,),
        in_specs=[pl.BlockSpec((tm, d), lambda i: (i, 0)), _resident((1, d)),
                  _resident(wg.shape), _resident(wu.shape), _resident(wd.shape)] + side_specs_in,
        out_specs=[pl.BlockSpec((tm, d), lambda i: (i, 0))] + side_specs_out,
        out_shape=[jax.ShapeDtypeStruct((n, d), F32)] + side_shapes,
        compiler_params=_cparams("parallel"),
        name="dense_ffn",
    )(x2, gain, wg, wu, wd, *side_in)
    return outs[0], [o.reshape(s) for o, s in zip(outs[1:], side_final)]


def _router_kernel(n_experts, stride, x_ref, g_ref, wr_ref, tril_ref, h_ref, idx_ref, wt_ref, total_ref, count_ref):
    h = _rms(x_ref[...], g_ref[...])
    h_ref[...] = h.astype(h_ref.dtype)
    w = wr_ref[...]
    h1 = h.astype(BF16)
    h2 = (h - h1.astype(F32)).astype(BF16)
    h3 = (h - h1.astype(F32) - h2.astype(F32)).astype(BF16)
    a, b, c = _dot(h1, w), _dot(h2, w), _dot(h3, w)

    def term(x, k):
        return x if k == 0 else pltpu.roll(x, LANES - k * stride, 1)

    logits = ((term(a, 0) + (term(a, 1) + term(b, 0)))
              + (term(b, 1) + term(a, 2) + term(c, 0)))
    lane = lax.broadcasted_iota(jnp.int32, logits.shape, 1)
    logits = jnp.where(lane < n_experts, logits, NEG_BIG)
    m1 = jnp.max(logits, axis=-1, keepdims=True)
    i1 = jnp.min(jnp.where(logits == m1, lane, LANES), axis=-1, keepdims=True)
    rest = jnp.where(lane == i1, NEG_BIG, logits)
    m2 = jnp.max(rest, axis=-1, keepdims=True)
    i2 = jnp.min(jnp.where(rest == m2, lane, LANES), axis=-1, keepdims=True)
    e2 = jnp.exp(m2 - m1)
    wt1 = 1.0 / (1.0 + e2)
    wt2 = e2 / (1.0 + e2)
    wt_ref[...] = jnp.where(lane == 0, wt1, jnp.where(lane == 1, wt2, 0.0))

    @pl.when(pl.program_id(0) == 0)
    def _():
        count_ref[...] = jnp.zeros_like(count_ref)

    chosen = jnp.where((lane == i1) | (lane == i2), 1.0, 0.0)
    before = _dot(tril_ref[...], chosen.astype(BF16)) + count_ref[...]
    r1 = jnp.sum(jnp.where(lane == i1, before, 0.0), axis=-1, keepdims=True).astype(jnp.int32)
    r2 = jnp.sum(jnp.where(lane == i2, before, 0.0), axis=-1, keepdims=True).astype(jnp.int32)
    count_ref[...] = count_ref[...] + jnp.sum(chosen, axis=0, keepdims=True)
    total_ref[...] = count_ref[...]
    route = jnp.where(lane == 0, i1, jnp.where(lane == 1, i2,
                      jnp.where(lane == 2, r1, jnp.where(lane == 3, r2, 0))))
    idx_ref[...] = jnp.transpose(route)[:idx_ref.shape[0], :]


def _router(x2, gain, w_router):
    n, d = x2.shape
    n_experts = w_router.shape[1]
    tm = min(ROW_TILE, n)
    stride = SUBLANES * (-(-n_experts // SUBLANES))
    assert 3 * stride <= LANES
    w = w_router.astype(F32)
    w1 = w.astype(BF16)
    w2 = (w - w1.astype(F32)).astype(BF16)
    w3 = (w - w1.astype(F32) - w2.astype(F32)).astype(BF16)
    wr = jnp.zeros((d, LANES), BF16)
    for k, wk in enumerate((w1, w2, w3)):
        wr = wr.at[:, k * stride:k * stride + n_experts].set(wk)
    tril = jnp.asarray(np.tril(np.ones((tm, tm), np.float32), -1), BF16)
    row = lambda i: (i, 0)
    return pl.pallas_call(
        functools.partial(_router_kernel, n_experts, stride),
        grid=(n // tm,),
        in_specs=[pl.BlockSpec((tm, d), row), _resident((1, d)), _resident((d, LANES)), _resident((tm, tm))],
        out_specs=[pl.BlockSpec((tm, d), row), pl.BlockSpec((SUBLANES, tm), lambda i: (0, i)),
                   pl.BlockSpec((tm, LANES), row), pl.BlockSpec((1, LANES), lambda i: (0, 0))],
        out_shape=[jax.ShapeDtypeStruct((n, d), BF16), jax.ShapeDtypeStruct((SUBLANES, n), jnp.int32),
                   jax.ShapeDtypeStruct((n, LANES), F32), jax.ShapeDtypeStruct((1, LANES), F32)],
        scratch_shapes=[pltpu.VMEM((1, LANES), F32)],
        compiler_params=_cparams("arbitrary"),
        name="router",
    )(x2, gain, wr, tril)


def _moe_kernel(ff_chunk, te_ref, tv_ref, x_ref, wg_ref, wu_ref, wd_ref, o_ref):
    t = pl.program_id(0)

    @pl.when(tv_ref[t] > 0)
    def _():
        y = _swiglu_tile(x_ref[...], wg_ref.at[0], wu_ref.at[0], wd_ref.at[0], ff_chunk)
        o_ref[...] = y.astype(o_ref.dtype)

    @pl.when(tv_ref[t] == 0)
    def _():
        o_ref[...] = jnp.zeros_like(o_ref)


def _moe_experts(xs, tile_expert, tile_valid, wg, wu, wd):
    p, d = xs.shape
    d_ff = wg.shape[2]
    tm = MOE_TILE
    grid_spec = pltpu.PrefetchScalarGridSpec(
        num_scalar_prefetch=2,
        grid=(p // tm,),
        in_specs=[pl.BlockSpec((tm, d), lambda t, te, tv: (t, 0)),
                  pl.BlockSpec((1, d, d_ff), lambda t, te, tv: (te[t], 0, 0)),
                  pl.BlockSpec((1, d, d_ff), lambda t, te, tv: (te[t], 0, 0)),
                  pl.BlockSpec((1, d_ff, d), lambda t, te, tv: (te[t], 0, 0))],
        out_specs=pl.BlockSpec((tm, d), lambda t, te, tv: (t, 0)),
    )
    return pl.pallas_call(
        functools.partial(_moe_kernel, _ffn_chunk(d_ff)),
        grid_spec=grid_spec,
        out_shape=jax.ShapeDtypeStruct((p, d), BF16),
        compiler_params=_cparams("arbitrary"),
        name="moe_experts",
    )(tile_expert, tile_valid, xs, wg, wu, wd)


def _combine_kernel(x_ref, wt_ref, y0_ref, y1_ref, o_ref):
    wt = wt_ref[...]
    o_ref[...] = x_ref[...] + (wt[:, 0:1] * y0_ref[...].astype(F32) + wt[:, 1:2] * y1_ref[...].astype(F32))


def _combine(x2, wt, y0, y1):
    n, d = x2.shape
    tm = min(ROW_TILE, n)
    row = lambda i: (i, 0)
    return pl.pallas_call(
        _combine_kernel,
        grid=(n // tm,),
        in_specs=[pl.BlockSpec((tm, d), row), pl.BlockSpec((tm, LANES), row),
                  pl.BlockSpec((tm, d), row), pl.BlockSpec((tm, d), row)],
        out_specs=pl.BlockSpec((tm, d), row),
        out_shape=jax.ShapeDtypeStruct((n, d), F32),
        compiler_params=_cparams("parallel"),
        name="moe_combine",
    )(x2, wt, y0, y1)


def _moe_ffn(x2, gain, w_router, expert_weights):
    n, d = x2.shape
    n_experts = w_router.shape[1]
    tm = MOE_TILE
    h, route, top_w, totals = _router(x2, gain, w_router)

    sizes = totals[0, :n_experts].astype(jnp.int32)
    padded = ((sizes + tm - 1) // tm) * tm
    starts = jnp.cumsum(padded) - padded
    pos = []
    for k in range(TOP_K):
        expert, rank = route[k], route[TOP_K + k]
        base = jnp.zeros_like(rank)
        for e in range(n_experts):
            base = base + jnp.where(expert == e, starts[e], 0)
        pos.append(rank + base)
    n_tiles = (n * TOP_K) // tm + n_experts
    p = n_tiles * tm
    token = jnp.tile(jnp.arange(n, dtype=jnp.int32), TOP_K)
    row_src = jnp.zeros((p,), jnp.int32).at[jnp.concatenate(pos)].set(
        token, unique_indices=True, mode="promise_in_bounds")
    tile_start = jnp.arange(n_tiles, dtype=jnp.int32) * tm
    ends = starts + padded
    tile_expert = jnp.minimum(jnp.sum((tile_start[:, None] >= ends[None, :]).astype(jnp.int32), axis=1),
                              n_experts - 1).astype(jnp.int32)
    tile_valid = (tile_start < ends[-1]).astype(jnp.int32)

    xs = h.at[row_src].get(mode="promise_in_bounds")
    ys = _moe_experts(xs, tile_expert, tile_valid, *expert_weights)
    y0 = ys.at[pos[0]].get(mode="promise_in_bounds")
    y1 = ys.at[pos[1]].get(mode="promise_in_bounds")
    return _combine(x2, top_w, y0, y1)


def kernel(x, attn_norm, w_in, b_gate, q_norm, k_norm, sinks, conv_w, conv_b, lru_w_r, lru_b_r, lru_w_i, lru_b_i, lru_lambda, w_proj_a, w_proj_b, w_proj_c, w_out, ffn_norm, w_ffn_gate, w_ffn_up, w_ffn_down, w_router, w_exp_gate, w_exp_up, w_exp_down):
    batch, seq, d = x.shape
    depth = w_in.shape[0]
    swa_q = w_proj_a.shape[1]
    sb_w = w_proj_b.shape[1]
    lru_w = w_proj_c.shape[1]
    swa_kv = SWA_KV_HEADS * HEAD_DIM
    widths = (swa_q, swa_kv, swa_kv, sb_w, sb_w, sb_w, lru_w, lru_w, N_BRANCHES * d)
    assert sum(widths) == w_in.shape[2]
    assert seq % SB_QBLK == 0 and seq % WINDOW == 0

    x2 = x.reshape(batch * seq, d).astype(F32)
    for l in range(depth):
        gain = attn_norm[l].astype(F32).reshape(1, d)
        qa, ka, va, qb, kb, vb, xc, gc, gates = _in_proj(x2, gain, _to_bf16(w_in, l), widths,
                                                         q_norm[l], k_norm[l])
        o_a = _swa(qa, ka, va, sinks[l], batch, seq)
        o_b = _stick_breaking(qb, kb, vb, batch, seq)
        o_c = _rglru(xc, gc, conv_w[l], conv_b[l], lru_w_r[l], lru_b_r[l], lru_w_i[l], lru_b_i[l],
                     lru_lambda[l], batch, seq)
        x2 = _merge(x2, o_a, o_b, o_c, gates, b_gate[l],
                    _to_bf16(w_proj_a, l), _to_bf16(w_proj_b, l), _to_bf16(w_proj_c, l),
                    _to_bf16(w_out, l))
        fgain = ffn_norm[l].astype(F32).reshape(1, d)
        j = l // 2
        if l % 2 == 0:
            side = [(w, j) for w in (w_exp_gate, w_exp_up, w_exp_down)] if l + 1 < depth else []
            x2, expert_bf16 = _dense_ffn(x2, fgain, _to_bf16(w_ffn_gate, j), _to_bf16(w_ffn_up, j),
                                         _to_bf16(w_ffn_down, j), side)
        else:
            x2 = _moe_ffn(x2, fgain, w_router[j], expert_bf16)
    return x2.reshape(batch, seq, d).astype(x.dtype)
```

```python
import functools

import numpy as np
import jax
import jax.numpy as jnp
from jax import lax
from jax.experimental import pallas as pl
from jax.experimental.pallas import tpu as pltpu

F32 = jnp.float32
BF16 = jnp.bfloat16

HEAD_DIM = 64
SWA_KV_HEADS = 2
WINDOW = 128
LRU_HEADS = 8
LRU_C = 8.0
CONV_W = 4
N_BRANCHES = 3
TOP_K = 2
EPS = 1e-6

LANES = 128
SUBLANES = 8
VMEM_LIMIT = 56 * 1024 * 1024
NEG_BIG = -1e30

SWA_BLOCKS_PER_STEP = 4
SB_QBLK = 256
SB_PAIRS_PER_STEP = 4
SB_UNDERFLOW = 105.0
LRU_CHUNK = 512
ROW_TILE = 512
MOE_TILE = 512


def _cparams(*sem):
    return pltpu.CompilerParams(dimension_semantics=sem, vmem_limit_bytes=VMEM_LIMIT)


def _resident(shape):
    nd = len(shape)
    return pl.BlockSpec(shape, lambda *_: (0,) * nd, pipeline_mode=pl.Buffered(1))


def _dot(a, b):
    return jnp.dot(a, b, preferred_element_type=F32)


def _dot_nt(a, b):
    return lax.dot_general(a, b, (((1,), (1,)), ((), ())), preferred_element_type=F32)


def _split_dot(a, b_bf16):
    hi = a.astype(BF16)
    lo = (a - hi.astype(F32)).astype(BF16)
    return _dot(hi, b_bf16) + _dot(lo, b_bf16)


def _rms(x, gain):
    ms = jnp.mean(x * x, axis=-1, keepdims=True)
    return x * lax.rsqrt(ms + EPS) * gain


CAST_BLOCK_BYTES = 4 * 1024 * 1024


def _cast_kernel(x_ref, o_ref):
    o_ref[...] = x_ref[...].astype(o_ref.dtype)


def _to_bf16(w, layer):
    shape = w.shape[1:]
    cols = shape[-1]
    rows = int(np.prod(shape[:-1]))
    tr = rows
    while tr * cols * 4 > CAST_BLOCK_BYTES and tr % 32 == 0:
        tr //= 2
    out = pl.pallas_call(
        _cast_kernel,
        grid=(rows // tr,),
        in_specs=[pl.BlockSpec((None, tr, cols), lambda i: (layer, i, 0))],
        out_specs=pl.BlockSpec((tr, cols), lambda i: (i, 0)),
        out_shape=jax.ShapeDtypeStruct((rows, cols), BF16),
        compiler_params=_cparams("parallel"),
        name="cast_bf16",
    )(w.reshape(w.shape[0], rows, cols))
    return out.reshape(shape)


def _head_norm(xf, bd, gain):
    ms = _dot((xf * xf).astype(BF16), bd)
    return xf * lax.rsqrt(ms + EPS) * gain


def _in_proj_kernel(widths, x_ref, g_ref, w_ref, bdq_ref, bdk_ref, qg_ref, kg_ref, *out_refs):
    h = _rms(x_ref[...], g_ref[...]).astype(BF16)
    normed = ((bdq_ref, qg_ref), (bdk_ref, kg_ref))
    offs = [sum(widths[:i]) for i in range(len(widths))]
    accs = [_dot(h, w_ref[:, offs[i]:offs[i] + widths[i]]) for i in range(len(normed))]
    for idx in range(len(normed), len(widths)):
        ref, width, off = out_refs[idx], widths[idx], offs[idx]
        for c in range(0, width, 1024):
            cw = min(1024, width - c)
            ref[:, c:c + cw] = _dot(h, w_ref[:, off + c:off + c + cw]).astype(ref.dtype)
    for idx, (bd_ref, gain_ref) in enumerate(normed):
        out_refs[idx][...] = _head_norm(accs[idx], bd_ref[...], gain_ref[...]).astype(out_refs[idx].dtype)


def _block_diag_mean(width):
    idx = np.arange(width) // HEAD_DIM
    return jnp.asarray((idx[:, None] == idx[None, :]).astype(np.float32) / HEAD_DIM, BF16)


def _in_proj(x2, gain, w_bf16, widths, q_norm, k_norm):
    n, d = x2.shape
    tm = min(ROW_TILE, n)
    qw, kw = widths[0], widths[1]
    qg = (jnp.tile(q_norm.astype(F32), qw // HEAD_DIM) * (HEAD_DIM ** -0.5)).reshape(1, qw)
    kg = jnp.tile(k_norm.astype(F32), kw // HEAD_DIM).reshape(1, kw)
    return pl.pallas_call(
        functools.partial(_in_proj_kernel, widths),
        grid=(n // tm,),
        in_specs=[pl.BlockSpec((tm, d), lambda i: (i, 0)),
                  _resident((1, d)),
                  _resident(w_bf16.shape),
                  _resident((qw, qw)), _resident((kw, kw)),
                  _resident((1, qw)), _resident((1, kw))],
        out_specs=[pl.BlockSpec((tm, w), lambda i: (i, 0)) for w in widths],
        out_shape=[jax.ShapeDtypeStruct((n, w), BF16) for w in widths],
        compiler_params=_cparams("parallel"),
        name="in_proj",
    )(x2, gain, w_bf16, _block_diag_mean(qw), _block_diag_mean(kw), qg, kg)


def _swa_kernel(n_heads, blk, q_ref, kc_ref, kp_ref, vc_ref, vp_ref, sink_ref, o_ref):
    n = pl.program_id(1)
    n_blocks = q_ref.shape[0] // blk
    group = n_heads // SWA_KV_HEADS

    upper = (lax.broadcasted_iota(jnp.int32, (blk, blk), 1)
             > lax.broadcasted_iota(jnp.int32, (blk, blk), 0))
    no_prev = upper & (n == 0)
    lane = lax.broadcasted_iota(jnp.int32, (blk, LANES), 1)
    low = lane < HEAD_DIM

    def swapped(t):
        return (t, pltpu.roll(t, HEAD_DIM, 1))

    rows = [slice(b * blk, (b + 1) * blk) for b in range(n_blocks)]
    k_blocks = [swapped(kp_ref[...])] + [swapped(kc_ref[r, :]) for r in rows]
    v_blocks = [swapped(vp_ref[...])] + [swapped(vc_ref[r, :]) for r in rows]
    kt_blocks = [tuple(jnp.transpose(t) for t in pair) for pair in k_blocks]

    jobs = [(b, j) for b in range(n_blocks) for j in range(n_heads)]
    sel = [(j // group + j % 2) % 2 for j in range(n_heads)]
    s_prev, s_cur, p_prev, p_cur, denom, outs = {}, {}, {}, {}, {}, {}

    def scores(t):
        b, j = jobs[t]
        q_pair = q_ref[rows[b], (j // 2) * LANES:(j // 2 + 1) * LANES]
        qm = jnp.where(low if j % 2 == 0 else ~low, q_pair, jnp.zeros_like(q_pair))
        s_prev[t] = _dot(qm, kt_blocks[b][sel[j]])
        s_cur[t] = _dot(qm, kt_blocks[b + 1][sel[j]])

    def softmax(t):
        b, j = jobs[t]
        s = jnp.where(upper, s_prev[t], s_cur[t])
        if b == 0:
            s = jnp.where(no_prev, NEG_BIG, s)
        sink = sink_ref[j]
        m = jnp.maximum(jnp.max(s, axis=-1, keepdims=True), sink)
        p = jnp.exp(s - m)
        denom[t] = jnp.sum(p, axis=-1, keepdims=True) + jnp.exp(sink - m)
        p_prev[t] = jnp.where(upper, p, 0.0).astype(BF16)
        p_cur[t] = jnp.where(upper, 0.0, p).astype(BF16)

    def values(t):
        b, j = jobs[t]
        outs[t] = (_dot(p_prev[t], v_blocks[b][sel[j]]), _dot(p_cur[t], v_blocks[b + 1][sel[j]]))

    def finish(t):
        b, j = jobs[t]
        outs[t] = (outs[t][0] + outs[t][1]) / denom[t]
        if j % 2 == 1:
            o_ref[rows[b], (j // 2) * LANES:(j // 2 + 1) * LANES] = jnp.where(
                low, outs[t - 1], outs[t]).astype(o_ref.dtype)

    for stage in (scores, softmax, values, finish):
        for t in range(len(jobs)):
            stage(t)


def _swa(qa, ka, va, sinks, batch, seq):
    n, qw = qa.shape
    kw = ka.shape[1]
    n_heads = qw // HEAD_DIM
    blk = WINDOW
    per_step = min(SWA_BLOCKS_PER_STEP, seq // blk)
    assert seq % (per_step * blk) == 0
    rows = per_step * blk
    ns = seq // rows
    cur = lambda b, i: (b * ns + i, 0)
    prev = lambda b, i: (b * ns * per_step + jnp.maximum(i * per_step - 1, 0), 0)
    return pl.pallas_call(
        functools.partial(_swa_kernel, n_heads, blk),
        grid=(batch, ns),
        in_specs=[pl.BlockSpec((rows, qw), cur),
                  pl.BlockSpec((rows, kw), cur), pl.BlockSpec((blk, kw), prev),
                  pl.BlockSpec((rows, kw), cur), pl.BlockSpec((blk, kw), prev),
                  pl.BlockSpec(memory_space=pltpu.SMEM)],
        out_specs=pl.BlockSpec((rows, qw), cur),
        out_shape=jax.ShapeDtypeStruct((n, qw), BF16),
        compiler_params=_cparams("parallel", "parallel"),
        name="swa",
    )(qa, ka, ka, va, va, sinks.astype(F32))


def _sb_kernel(q_ref, k_ref, v_ref, tri_ref, o_ref):
    i = pl.program_id(2)
    qb, width = q_ref.shape
    n_heads = 2 * (width // LANES)
    lane = lax.broadcasted_iota(jnp.int32, (qb, LANES), 1)
    low = lane < HEAD_DIM
    scale = jnp.asarray(HEAD_DIM ** -0.5, q_ref.dtype)
    q_heads = []
    for h in range(n_heads):
        q_pair = q_ref[:, (h // 2) * LANES:(h // 2 + 1) * LANES]
        q_heads.append(jnp.where(low if h % 2 == 0 else ~low, q_pair, jnp.zeros_like(q_pair)) * scale)
    tri = tri_ref[...]
    strict = (lax.broadcasted_iota(jnp.int32, (qb, qb), 1)
              < lax.broadcasted_iota(jnp.int32, (qb, qb), 0))

    def step(chunks, carries, accs):
        carries, accs = list(carries), list(accs)
        jobs = [(ci, h) for ci in range(len(chunks)) for h in range(n_heads)]
        z, log_beta, fail, fail_sum, later, w = {}, {}, {}, {}, {}, {}

        def cols(ref, ci, h):
            start = pl.multiple_of(chunks[ci][0] * qb, qb)
            return ref[pl.ds(start, qb), (h // 2) * LANES:(h // 2 + 1) * LANES]

        def scores(j):
            ci, h = jobs[j]
            z[j] = _dot_nt(q_heads[h], cols(k_ref, ci, h))

        def softplus(j):
            sp = jnp.maximum(z[j], 0.0) + jnp.log(1.0 + jnp.exp(-jnp.abs(z[j])))
            log_beta[j] = z[j] - sp
            spm = jnp.where(strict, sp, 0.0) if chunks[jobs[j][0]][1] else sp
            fail[j] = spm.astype(BF16)
            fail_sum[j] = jnp.sum(spm, axis=-1, keepdims=True)

        def suffix_sums(j):
            later[j] = _dot(fail[j], tri)

        def weights(j):
            ci, h = jobs[j]
            wj = jnp.exp(log_beta[j] - (later[j] + carries[h]))
            w[j] = (jnp.where(strict, wj, 0.0) if chunks[ci][1] else wj).astype(BF16)
            carries[h] = carries[h] + fail_sum[j]

        def values(j):
            ci, h = jobs[j]
            accs[h] = accs[h] + _dot(w[j], cols(v_ref, ci, h))

        for stage in (scores, softplus, suffix_sums, weights, values):
            for j in range(len(jobs)):
                stage(j)
        return tuple(carries), tuple(accs)

    carries = (jnp.zeros((qb, 1), F32),) * n_heads
    accs = (jnp.zeros((qb, LANES), F32),) * n_heads
    carries, accs = lax.cond(i == 0,
                             lambda: step([(i, True)], carries, accs),
                             lambda: step([(i, True), (i - 1, False)], carries, accs))

    def cond(state):
        t, carries, _ = state
        lowest = functools.reduce(jnp.minimum, [jnp.min(c) for c in carries])
        return (t < i - 1) & (lowest <= SB_UNDERFLOW)

    def body(state):
        t, carries, accs = state
        carries, accs = step([(i - 2 - t, False)], carries, accs)
        return t + 1, carries, accs

    _, _, accs = lax.while_loop(cond, body, (jnp.int32(0), carries, accs))
    for p in range(width // LANES):
        o_ref[:, p * LANES:(p + 1) * LANES] = jnp.where(low, accs[2 * p], accs[2 * p + 1]).astype(o_ref.dtype)


def _stick_breaking(qb, kb, vb, batch, seq):
    n, w = qb.shape
    blk = min(SB_QBLK, seq)
    nq = seq // blk
    cols = SB_PAIRS_PER_STEP * LANES
    tri = jnp.asarray(np.tril(np.ones((blk, blk), np.float32), -1), BF16)
    return pl.pallas_call(
        _sb_kernel,
        grid=(batch, w // cols, nq),
        in_specs=[pl.BlockSpec((blk, cols), lambda b, p, i: (b * nq + i, p)),
                  pl.BlockSpec((seq, cols), lambda b, p, i: (b, p)),
                  pl.BlockSpec((seq, cols), lambda b, p, i: (b, p)),
                  _resident((blk, blk))],
        out_specs=pl.BlockSpec((blk, cols), lambda b, p, i: (b * nq + i, p)),
        out_shape=jax.ShapeDtypeStruct((n, w), BF16),
        compiler_params=_cparams("parallel", "parallel", "arbitrary"),
        name="stick_breaking",
    )(qb, kb, vb, tri)


def _gelu_tanh(x):
    return 0.5 * x * (1.0 + jnp.tanh(np.sqrt(2.0 / np.pi) * (x + 0.044715 * (x * x * x))))


def _lru_kernel(x_ref, g_ref, cw_ref, cb_ref, wri_ref, bri_ref, lam_ref, o_ref, tail_ref, h_ref):
    j = pl.program_id(1)
    ts, width = x_ref.shape

    @pl.when(j == 0)
    def _():
        tail_ref[...] = jnp.zeros_like(tail_ref)
        h_ref[...] = jnp.zeros_like(h_ref)

    x = x_ref[...].astype(F32)
    tail = tail_ref[...]
    row8 = lax.broadcasted_iota(jnp.int32, (SUBLANES, width), 0)
    y = x * cw_ref[CONV_W - 1:CONV_W, :] + cb_ref[...]
    for d in range(1, CONV_W):
        xs = pltpu.roll(x, d, 0)
        top = jnp.where(row8 < d, pltpu.roll(tail, d, 0), xs[:SUBLANES])
        xs = jnp.concatenate([top, xs[SUBLANES:]], axis=0)
        y = y + xs * cw_ref[CONV_W - 1 - d:CONV_W - d, :]
    tail_ref[...] = x[ts - SUBLANES:]

    ri = _dot(y.astype(BF16), wri_ref[...]) + bri_ref[...]
    r = jax.nn.sigmoid(ri[:, :width])
    gate_i = jax.nn.sigmoid(ri[:, width:])
    lam = lam_ref[...]
    softplus_neg_lam = jnp.maximum(-lam, 0.0) + jnp.log(1.0 + jnp.exp(-jnp.abs(lam)))
    a = jnp.exp2(r * ((-LRU_C * np.log2(np.e)) * softplus_neg_lam))
    u = jnp.sqrt(1.0 - a * a) * (gate_i * y)

    grouped = (ts // SUBLANES, SUBLANES, width)
    a, u = a.reshape(grouped), u.reshape(grouped)
    in_group = lax.broadcasted_iota(jnp.int32, grouped, 1)
    d = 1
    while d < SUBLANES:
        keep = in_group >= d
        a_sh = jnp.where(keep, pltpu.roll(a, d, 1), 1.0)
        u_sh = jnp.where(keep, pltpu.roll(u, d, 1), 0.0)
        u = a * u_sh + u
        a = a * a_sh
        d *= 2
    a, u = a.reshape(ts, width), u.reshape(ts, width)
    state = h_ref[...]
    groups = []
    for g in range(ts // SUBLANES):
        rows = slice(g * SUBLANES, (g + 1) * SUBLANES)
        hg = u[rows] + a[rows] * state
        groups.append(hg)
        state = hg[SUBLANES - 1:SUBLANES, :]
    h = jnp.concatenate(groups, axis=0)
    h_ref[...] = state
    o_ref[...] = (h * _gelu_tanh(g_ref[...].astype(F32))).astype(o_ref.dtype)


def _block_diag(w):
    heads, blk, _ = w.shape
    eye = jnp.eye(heads, dtype=w.dtype)
    return jnp.einsum('hij,hg->higj', w, eye).reshape(heads * blk, heads * blk)


def _rglru(xc, gc, conv_w, conv_b, w_r, b_r, w_i, b_i, lam, batch, seq):
    n, width = xc.shape
    ts = min(LRU_CHUNK, seq)
    nc = seq // ts
    wri = jnp.concatenate([_block_diag(w_r), _block_diag(w_i)], axis=1).astype(BF16)
    bri = jnp.concatenate([b_r, b_i]).astype(F32).reshape(1, 2 * width)
    blk = lambda b, j: (b * nc + j, 0)
    return pl.pallas_call(
        _lru_kernel,
        grid=(batch, nc),
        in_specs=[pl.BlockSpec((ts, width), blk), pl.BlockSpec((ts, width), blk),
                  _resident((CONV_W, width)), _resident((1, width)),
                  _resident((width, 2 * width)), _resident((1, 2 * width)),
                  _resident((1, width))],
        out_specs=pl.BlockSpec((ts, width), blk),
        out_shape=jax.ShapeDtypeStruct((n, width), BF16),
        scratch_shapes=[pltpu.VMEM((SUBLANES, width), F32), pltpu.VMEM((1, width), F32)],
        compiler_params=_cparams("parallel", "arbitrary"),
        name="rglru",
    )(xc, gc, conv_w.astype(F32), conv_b.astype(F32).reshape(1, width), wri, bri,
      lam.astype(F32).reshape(1, width))


def _merge_kernel(x_ref, oa_ref, ob_ref, oc_ref, gt_ref, bg_ref, wa_ref, wb_ref, wc_ref, wo_ref, o_ref):
    tm, d = x_ref.shape
    branches = ((oa_ref, wa_ref), (ob_ref, wb_ref), (oc_ref, wc_ref))
    halves = [slice(r * (tm // 2), (r + 1) * (tm // 2)) for r in range(2)]
    proj = [[_dot(b_ref[rows, :], w_ref[...]) for b_ref, w_ref in branches] for rows in halves]
    for rows, terms in zip(halves, proj):
        merged = None
        for idx, term in enumerate(terms):
            cols = slice(idx * d, (idx + 1) * d)
            gate = jax.nn.sigmoid(gt_ref[rows, cols].astype(F32) + bg_ref[:, cols])
            merged = gate * term if merged is None else merged + gate * term
        o_ref[rows, :] = x_ref[rows, :] + _dot(merged.astype(BF16), wo_ref[...])


def _merge(x2, o_a, o_b, o_c, gates, b_gate, wa, wb, wc, wo):
    n, d = x2.shape
    tm = min(ROW_TILE, n)
    row = lambda i: (i, 0)
    return pl.pallas_call(
        _merge_kernel,
        grid=(n // tm,),
        in_specs=[pl.BlockSpec((tm, d), row),
                  pl.BlockSpec((tm, o_a.shape[1]), row), pl.BlockSpec((tm, o_b.shape[1]), row),
                  pl.BlockSpec((tm, o_c.shape[1]), row), pl.BlockSpec((tm, gates.shape[1]), row),
                  _resident((1, gates.shape[1])),
                  _resident(wa.shape), _resident(wb.shape), _resident(wc.shape), _resident(wo.shape)],
        out_specs=pl.BlockSpec((tm, d), row),
        out_shape=jax.ShapeDtypeStruct((n, d), F32),
        compiler_params=_cparams("parallel"),
        name="merge",
    )(x2, o_a, o_b, o_c, gates, b_gate.astype(F32).reshape(1, -1), wa, wb, wc, wo)


def _swiglu_tile(h, wg_ref, wu_ref, wd_ref, ff_chunk):
    d_ff = wg_ref.shape[-1]
    starts = list(range(0, d_ff, ff_chunk))

    def gate_up(c):
        return _dot(h, wg_ref[:, c:c + ff_chunk]), _dot(h, wu_ref[:, c:c + ff_chunk])

    acc = None
    pending = gate_up(starts[0])
    for i, c in enumerate(starts):
        g, u = pending
        if i + 1 < len(starts):
            pending = gate_up(starts[i + 1])
        act = (g * jax.nn.sigmoid(g) * u).astype(BF16)
        part = _dot(act, wd_ref[c:c + ff_chunk, :])
        acc = part if acc is None else acc + part
    return acc


def _ffn_chunk(d_ff):
    for c in (512, 896, 256, 128):
        if d_ff % c == 0:
            return c
    return d_ff


def _dense_ffn_kernel(ff_chunk, n_side, x_ref, g_ref, wg_ref, wu_ref, wd_ref, *refs):
    side_in, o_ref, side_out = refs[:n_side], refs[n_side], refs[n_side + 1:]
    x = x_ref[...]
    h = _rms(x, g_ref[...]).astype(BF16)
    o_ref[...] = x + _swiglu_tile(h, wg_ref, wu_ref, wd_ref, ff_chunk)
    for src, dst in zip(side_in, side_out):
        dst[...] = src[...].astype(dst.dtype)


def _dense_ffn(x2, gain, wg, wu, wd, side_casts=()):
    n, d = x2.shape
    tm = min(ROW_TILE, n)
    steps = n // tm
    side_in, side_specs_in, side_specs_out, side_shapes, side_final = [], [], [], [], []
    for w, layer in side_casts:
        shape = w.shape[1:]
        cols = shape[-1]
        rows = int(np.prod(shape[:-1]))
        assert rows % (steps * 2 * SUBLANES) == 0
        tr = rows // steps
        side_in.append(w.reshape(w.shape[0], rows, cols))
        side_specs_in.append(pl.BlockSpec((None, tr, cols), lambda i, layer=layer: (layer, i, 0)))
        side_specs_out.append(pl.BlockSpec((tr, cols), lambda i: (i, 0)))
        side_shapes.append(jax.ShapeDtypeStruct((rows, cols), BF16))
        side_final.append(shape)
    outs = pl.pallas_call(
        functools.partial(_dense_ffn_kernel, _ffn_chunk(wg.shape[1]), len(side_casts)),
        grid=(steps,),
        in_specs=[pl.BlockSpec((tm, d), lambda i: (i, 0)), _resident((1, d)),
                  _resident(wg.shape), _resident(wu.shape), _resident(wd.shape)] + side_specs_in,
        out_specs=[pl.BlockSpec((tm, d), lambda i: (i, 0))] + side_specs_out,
        out_shape=[jax.ShapeDtypeStruct((n, d), F32)] + side_shapes,
        compiler_params=_cparams("parallel"),
        name="dense_ffn",
    )(x2, gain, wg, wu, wd, *side_in)
    return outs[0], [o.reshape(s) for o, s in zip(outs[1:], side_final)]


def _router_kernel(n_experts, stride, x_ref, g_ref, wr_ref, tril_ref, h_ref, idx_ref, wt_ref, total_ref, count_ref):
    h = _rms(x_ref[...], g_ref[...])
    h_ref[...] = h.astype(h_ref.dtype)
    w = wr_ref[...]
    h1 = h.astype(BF16)
    h2 = (h - h1.astype(F32)).astype(BF16)
    h3 = (h - h1.astype(F32) - h2.astype(F32)).astype(BF16)
    a, b, c = _dot(h1, w), _dot(h2, w), _dot(h3, w)

    def term(x, k):
        return x if k == 0 else pltpu.roll(x, LANES - k * stride, 1)

    logits = ((term(a, 0) + (term(a, 1) + term(b, 0)))
              + (term(b, 1) + term(a, 2) + term(c, 0)))
    lane = lax.broadcasted_iota(jnp.int32, logits.shape, 1)
    logits = jnp.where(lane < n_experts, logits, NEG_BIG)
    m1 = jnp.max(logits, axis=-1, keepdims=True)
    i1 = jnp.min(jnp.where(logits == m1, lane, LANES), axis=-1, keepdims=True)
    rest = jnp.where(lane == i1, NEG_BIG, logits)
    m2 = jnp.max(rest, axis=-1, keepdims=True)
    i2 = jnp.min(jnp.where(rest == m2, lane, LANES), axis=-1, keepdims=True)
    e2 = jnp.exp(m2 - m1)
    wt1 = 1.0 / (1.0 + e2)
    wt2 = e2 / (1.0 + e2)
    wt_ref[...] = jnp.where(lane == 0, wt1, jnp.where(lane == 1, wt2, 0.0))

    @pl.when(pl.program_id(0) == 0)
    def _():
        count_ref[...] = jnp.zeros_like(count_ref)

    chosen = jnp.where((lane == i1) | (lane == i2), 1.0, 0.0)
    before = _dot(tril_ref[...], chosen.astype(BF16)) + count_ref[...]
    r1 = jnp.sum(jnp.where(lane == i1, before, 0.0), axis=-1, keepdims=True).astype(jnp.int32)
    r2 = jnp.sum(jnp.where(lane == i2, before, 0.0), axis=-1, keepdims=True).astype(jnp.int32)
    count_ref[...] = count_ref[...] + jnp.sum(chosen, axis=0, keepdims=True)
    total_ref[...] = count_ref[...]
    route = jnp.where(lane == 0, i1, jnp.where(lane == 1, i2,
                      jnp.where(lane == 2, r1, jnp.where(lane == 3, r2, 0))))
    idx_ref[...] = jnp.transpose(route)[:idx_ref.shape[0], :]


def _router(x2, gain, w_router):
    n, d = x2.shape
    n_experts = w_router.shape[1]
    tm = min(ROW_TILE, n)
    stride = SUBLANES * (-(-n_experts // SUBLANES))
    assert 3 * stride <= LANES
    w = w_router.astype(F32)
    w1 = w.astype(BF16)
    w2 = (w - w1.astype(F32)).astype(BF16)
    w3 = (w - w1.astype(F32) - w2.astype(F32)).astype(BF16)
    wr = jnp.zeros((d, LANES), BF16)
    for k, wk in enumerate((w1, w2, w3)):
        wr = wr.at[:, k * stride:k * stride + n_experts].set(wk)
    tril = jnp.asarray(np.tril(np.ones((tm, tm), np.float32), -1), BF16)
    row = lambda i: (i, 0)
    return pl.pallas_call(
        functools.partial(_router_kernel, n_experts, stride),
        grid=(n // tm,),
        in_specs=[pl.BlockSpec((tm, d), row), _resident((1, d)), _resident((d, LANES)), _resident((tm, tm))],
        out_specs=[pl.BlockSpec((tm, d), row), pl.BlockSpec((SUBLANES, tm), lambda i: (0, i)),
                   pl.BlockSpec((tm, LANES), row), pl.BlockSpec((1, LANES), lambda i: (0, 0))],
        out_shape=[jax.ShapeDtypeStruct((n, d), BF16), jax.ShapeDtypeStruct((SUBLANES, n), jnp.int32),
                   jax.ShapeDtypeStruct((n, LANES), F32), jax.ShapeDtypeStruct((1, LANES), F32)],
        scratch_shapes=[pltpu.VMEM((1, LANES), F32)],
        compiler_params=_cparams("arbitrary"),
        name="router",
    )(x2, gain, wr, tril)


def _moe_kernel(ff_chunk, te_ref, tv_ref, x_ref, wg_ref, wu_ref, wd_ref, o_ref):
    t = pl.program_id(0)

    @pl.when(tv_ref[t] > 0)
    def _():
        y = _swiglu_tile(x_ref[...], wg_ref.at[0], wu_ref.at[0], wd_ref.at[0], ff_chunk)
        o_ref[...] = y.astype(o_ref.dtype)

    @pl.when(tv_ref[t] == 0)
    def _():
        o_ref[...] = jnp.zeros_like(o_ref)


def _moe_experts(xs, tile_expert, tile_valid, wg, wu, wd):
    p, d = xs.shape
    d_ff = wg.shape[2]
    tm = MOE_TILE
    grid_spec = pltpu.PrefetchScalarGridSpec(
        num_scalar_prefetch=2,
        grid=(p // tm,),
        in_specs=[pl.BlockSpec((tm, d), lambda t, te, tv: (t, 0)),
                  pl.BlockSpec((1, d, d_ff), lambda t, te, tv: (te[t], 0, 0)),
                  pl.BlockSpec((1, d, d_ff), lambda t, te, tv: (te[t], 0, 0)),
                  pl.BlockSpec((1, d_ff, d), lambda t, te, tv: (te[t], 0, 0))],
        out_specs=pl.BlockSpec((tm, d), lambda t, te, tv: (t, 0)),
    )
    return pl.pallas_call(
        functools.partial(_moe_kernel, _ffn_chunk(d_ff)),
        grid_spec=grid_spec,
        out_shape=jax.ShapeDtypeStruct((p, d), BF16),
        compiler_params=_cparams("arbitrary"),
        name="moe_experts",
    )(tile_expert, tile_valid, xs, wg, wu, wd)


def _combine_kernel(x_ref, wt_ref, y0_ref, y1_ref, o_ref):
    wt = wt_ref[...]
    o_ref[...] = x_ref[...] + (wt[:, 0:1] * y0_ref[...].astype(F32) + wt[:, 1:2] * y1_ref[...].astype(F32))


def _combine(x2, wt, y0, y1):
    n, d = x2.shape
    tm = min(ROW_TILE, n)
    row = lambda i: (i, 0)
    return pl.pallas_call(
        _combine_kernel,
        grid=(n // tm,),
        in_specs=[pl.BlockSpec((tm, d), row), pl.BlockSpec((tm, LANES), row),
                  pl.BlockSpec((tm, d), row), pl.BlockSpec((tm, d), row)],
        out_specs=pl.BlockSpec((tm, d), row),
        out_shape=jax.ShapeDtypeStruct((n, d), F32),
        compiler_params=_cparams("parallel"),
        name="moe_combine",
    )(x2, wt, y0, y1)


def _moe_ffn(x2, gain, w_router, expert_weights):
    n, d = x2.shape
    n_experts = w_router.shape[1]
    tm = MOE_TILE
    h, route, top_w, totals = _router(x2, gain, w_router)

    sizes = totals[0, :n_experts].astype(jnp.int32)
    padded = ((sizes + tm - 1) // tm) * tm
    starts = jnp.cumsum(padded) - padded
    pos = []
    for k in range(TOP_K):
        expert, rank = route[k], route[TOP_K + k]
        base = jnp.zeros_like(rank)
        for e in range(n_experts):
            base = base + jnp.where(expert == e, starts[e], 0)
        pos.append(rank + base)
    n_tiles = (n * TOP_K) // tm + n_experts
    p = n_tiles * tm
    token = jnp.tile(jnp.arange(n, dtype=jnp.int32), TOP_K)
    row_src = (jnp.arange(p, dtype=jnp.int32) % n).at[jnp.concatenate(pos)].set(
        token, unique_indices=True, mode="promise_in_bounds")
    tile_start = jnp.arange(n_tiles, dtype=jnp.int32) * tm
    ends = starts + padded
    tile_expert = jnp.minimum(jnp.sum((tile_start[:, None] >= ends[None, :]).astype(jnp.int32), axis=1),
                              n_experts - 1).astype(jnp.int32)
    tile_valid = (tile_start < ends[-1]).astype(jnp.int32)

    xs = h.at[row_src].get(mode="promise_in_bounds")
    ys = _moe_experts(xs, tile_expert, tile_valid, *expert_weights)
    y0 = ys.at[pos[0]].get(mode="promise_in_bounds")
    y1 = ys.at[pos[1]].get(mode="promise_in_bounds")
    return _combine(x2, top_w, y0, y1)


def kernel(x, attn_norm, w_in, b_gate, q_norm, k_norm, sinks, conv_w, conv_b, lru_w_r, lru_b_r, lru_w_i, lru_b_i, lru_lambda, w_proj_a, w_proj_b, w_proj_c, w_out, ffn_norm, w_ffn_gate, w_ffn_up, w_ffn_down, w_router, w_exp_gate, w_exp_up, w_exp_down):
    batch, seq, d = x.shape
    depth = w_in.shape[0]
    swa_q = w_proj_a.shape[1]
    sb_w = w_proj_b.shape[1]
    lru_w = w_proj_c.shape[1]
    swa_kv = SWA_KV_HEADS * HEAD_DIM
    widths = (swa_q, swa_kv, swa_kv, sb_w, sb_w, sb_w, lru_w, lru_w, N_BRANCHES * d)
    assert sum(widths) == w_in.shape[2]
    assert seq % SB_QBLK == 0 and seq % WINDOW == 0

    x2 = x.reshape(batch * seq, d).astype(F32)
    for l in range(depth):
        gain = attn_norm[l].astype(F32).reshape(1, d)
        qa, ka, va, qb, kb, vb, xc, gc, gates = _in_proj(x2, gain, _to_bf16(w_in, l), widths,
                                                         q_norm[l], k_norm[l])
        o_a = _swa(qa, ka, va, sinks[l], batch, seq)
        o_b = _stick_breaking(qb, kb, vb, batch, seq)
        o_c = _rglru(xc, gc, conv_w[l], conv_b[l], lru_w_r[l], lru_b_r[l], lru_w_i[l], lru_b_i[l],
                     lru_lambda[l], batch, seq)
        x2 = _merge(x2, o_a, o_b, o_c, gates, b_gate[l],
                    _to_bf16(w_proj_a, l), _to_bf16(w_proj_b, l), _to_bf16(w_proj_c, l),
                    _to_bf16(w_out, l))
        fgain = ffn_norm[l].astype(F32).reshape(1, d)
        j = l // 2
        if l % 2 == 0:
            side = [(w, j) for w in (w_exp_gate, w_exp_up, w_exp_down)] if l + 1 < depth else []
            x2, expert_bf16 = _dense_ffn(x2, fgain, _to_bf16(w_ffn_gate, j), _to_bf16(w_ffn_up, j),
                                         _to_bf16(w_ffn_down, j), side)
        else:
            x2 = _moe_ffn(x2, fgain, w_router[j], expert_bf16)
    return x2.reshape(batch, seq, d).astype(x.dtype)
```

```python
import functools

import numpy as np
import jax
import jax.numpy as jnp
from jax import lax
from jax.experimental import pallas as pl
from jax.experimental.pallas import tpu as pltpu

F32 = jnp.float32
BF16 = jnp.bfloat16

HEAD_DIM = 64
SWA_KV_HEADS = 2
WINDOW = 128
LRU_HEADS = 8
LRU_C = 8.0
CONV_W = 4
N_BRANCHES = 3
TOP_K = 2
EPS = 1e-6

LANES = 128
SUBLANES = 8
VMEM_LIMIT = 56 * 1024 * 1024
NEG_BIG = -1e30

SWA_BLOCKS_PER_STEP = 4
SB_QBLK = 256
SB_PAIRS_PER_STEP = 4
SB_UNDERFLOW = 105.0
LRU_CHUNK = 512
ROW_TILE = 512
MOE_TILE = 512


def _cparams(*sem):
    return pltpu.CompilerParams(dimension_semantics=sem, vmem_limit_bytes=VMEM_LIMIT)


def _resident(shape):
    nd = len(shape)
    return pl.BlockSpec(shape, lambda *_: (0,) * nd, pipeline_mode=pl.Buffered(1))


def _dot(a, b):
    return jnp.dot(a, b, preferred_element_type=F32)


def _dot_nt(a, b):
    return lax.dot_general(a, b, (((1,), (1,)), ((), ())), preferred_element_type=F32)


def _split_dot(a, b_bf16):
    hi = a.astype(BF16)
    lo = (a - hi.astype(F32)).astype(BF16)
    return _dot(hi, b_bf16) + _dot(lo, b_bf16)


def _rms(x, gain):
    ms = jnp.mean(x * x, axis=-1, keepdims=True)
    return x * lax.rsqrt(ms + EPS) * gain


CAST_BLOCK_BYTES = 4 * 1024 * 1024


def _cast_kernel(x_ref, o_ref):
    o_ref[...] = x_ref[...].astype(o_ref.dtype)


def _to_bf16(w, layer):
    shape = w.shape[1:]
    cols = shape[-1]
    rows = int(np.prod(shape[:-1]))
    tr = rows
    while tr * cols * 4 > CAST_BLOCK_BYTES and tr % 32 == 0:
        tr //= 2
    out = pl.pallas_call(
        _cast_kernel,
        grid=(rows // tr,),
        in_specs=[pl.BlockSpec((None, tr, cols), lambda i: (layer, i, 0))],
        out_specs=pl.BlockSpec((tr, cols), lambda i: (i, 0)),
        out_shape=jax.ShapeDtypeStruct((rows, cols), BF16),
        compiler_params=_cparams("parallel"),
        name="cast_bf16",
    )(w.reshape(w.shape[0], rows, cols))
    return out.reshape(shape)


def _head_norm(xf, bd, gain):
    ms = _dot((xf * xf).astype(BF16), bd)
    return xf * lax.rsqrt(ms + EPS) * gain


def _in_proj_kernel(widths, x_ref, g_ref, w_ref, bdq_ref, bdk_ref, qg_ref, kg_ref, *out_refs):
    h = _rms(x_ref[...], g_ref[...]).astype(BF16)
    normed = ((bdq_ref, qg_ref), (bdk_ref, kg_ref))
    offs = [sum(widths[:i]) for i in range(len(widths))]
    accs = [_dot(h, w_ref[:, offs[i]:offs[i] + widths[i]]) for i in range(len(normed))]
    for idx in range(len(normed), len(widths)):
        ref, width, off = out_refs[idx], widths[idx], offs[idx]
        for c in range(0, width, 1024):
            cw = min(1024, width - c)
            ref[:, c:c + cw] = _dot(h, w_ref[:, off + c:off + c + cw]).astype(ref.dtype)
    for idx, (bd_ref, gain_ref) in enumerate(normed):
        out_refs[idx][...] = _head_norm(accs[idx], bd_ref[...], gain_ref[...]).astype(out_refs[idx].dtype)


def _block_diag_mean(width):
    idx = np.arange(width) // HEAD_DIM
    return jnp.asarray((idx[:, None] == idx[None, :]).astype(np.float32) / HEAD_DIM, BF16)


def _in_proj(x2, gain, w_bf16, widths, q_norm, k_norm):
    n, d = x2.shape
    tm = min(ROW_TILE, n)
    qw, kw = widths[0], widths[1]
    qg = (jnp.tile(q_norm.astype(F32), qw // HEAD_DIM) * (HEAD_DIM ** -0.5)).reshape(1, qw)
    kg = jnp.tile(k_norm.astype(F32), kw // HEAD_DIM).reshape(1, kw)
    return pl.pallas_call(
        functools.partial(_in_proj_kernel, widths),
        grid=(n // tm,),
        in_specs=[pl.BlockSpec((tm, d), lambda i: (i, 0)),
                  _resident((1, d)),
                  _resident(w_bf16.shape),
                  _resident((qw, qw)), _resident((kw, kw)),
                  _resident((1, qw)), _resident((1, kw))],
        out_specs=[pl.BlockSpec((tm, w), lambda i: (i, 0)) for w in widths],
        out_shape=[jax.ShapeDtypeStruct((n, w), BF16) for w in widths],
        compiler_params=_cparams("parallel"),
        name="in_proj",
    )(x2, gain, w_bf16, _block_diag_mean(qw), _block_diag_mean(kw), qg, kg)


def _swa_kernel(n_heads, blk, q_ref, kc_ref, kp_ref, vc_ref, vp_ref, sink_ref, o_ref):
    n = pl.program_id(1)
    n_blocks = q_ref.shape[0] // blk
    group = n_heads // SWA_KV_HEADS

    upper = (lax.broadcasted_iota(jnp.int32, (blk, blk), 1)
             > lax.broadcasted_iota(jnp.int32, (blk, blk), 0))
    no_prev = upper & (n == 0)
    lane = lax.broadcasted_iota(jnp.int32, (blk, LANES), 1)
    low = lane < HEAD_DIM

    def swapped(t):
        return (t, pltpu.roll(t, HEAD_DIM, 1))

    rows = [slice(b * blk, (b + 1) * blk) for b in range(n_blocks)]
    k_blocks = [swapped(kp_ref[...])] + [swapped(kc_ref[r, :]) for r in rows]
    v_blocks = [swapped(vp_ref[...])] + [swapped(vc_ref[r, :]) for r in rows]
    kt_blocks = [tuple(jnp.transpose(t) for t in pair) for pair in k_blocks]

    jobs = [(b, j) for b in range(n_blocks) for j in range(n_heads)]
    sel = [(j // group + j % 2) % 2 for j in range(n_heads)]
    s_prev, s_cur, p_prev, p_cur, denom, outs = {}, {}, {}, {}, {}, {}

    def scores(t):
        b, j = jobs[t]
        q_pair = q_ref[rows[b], (j // 2) * LANES:(j // 2 + 1) * LANES]
        qm = jnp.where(low if j % 2 == 0 else ~low, q_pair, jnp.zeros_like(q_pair))
        s_prev[t] = _dot(qm, kt_blocks[b][sel[j]])
        s_cur[t] = _dot(qm, kt_blocks[b + 1][sel[j]])

    def softmax(t):
        b, j = jobs[t]
        s = jnp.where(upper, s_prev[t], s_cur[t])
        if b == 0:
            s = jnp.where(no_prev, NEG_BIG, s)
        sink = sink_ref[j]
        m = jnp.maximum(jnp.max(s, axis=-1, keepdims=True), sink)
        p = jnp.exp(s - m)
        denom[t] = jnp.sum(p, axis=-1, keepdims=True) + jnp.exp(sink - m)
        p_prev[t] = jnp.where(upper, p, 0.0).astype(BF16)
        p_cur[t] = jnp.where(upper, 0.0, p).astype(BF16)

    def values(t):
        b, j = jobs[t]
        outs[t] = (_dot(p_prev[t], v_blocks[b][sel[j]]), _dot(p_cur[t], v_blocks[b + 1][sel[j]]))

    def finish(t):
        b, j = jobs[t]
        outs[t] = (outs[t][0] + outs[t][1]) / denom[t]
        if j % 2 == 1:
            o_ref[rows[b], (j // 2) * LANES:(j // 2 + 1) * LANES] = jnp.where(
                low, outs[t - 1], outs[t]).astype(o_ref.dtype)

    for stage in (scores, softmax, values, finish):
        for t in range(len(jobs)):
            stage(t)


def _swa(qa, ka, va, sinks, batch, seq):
    n, qw = qa.shape
    kw = ka.shape[1]
    n_heads = qw // HEAD_DIM
    blk = WINDOW
    per_step = min(SWA_BLOCKS_PER_STEP, seq // blk)
    assert seq % (per_step * blk) == 0
    rows = per_step * blk
    ns = seq // rows
    cur = lambda b, i: (b * ns + i, 0)
    prev = lambda b, i: (b * ns * per_step + jnp.maximum(i * per_step - 1, 0), 0)
    return pl.pallas_call(
        functools.partial(_swa_kernel, n_heads, blk),
        grid=(batch, ns),
        in_specs=[pl.BlockSpec((rows, qw), cur),
                  pl.BlockSpec((rows, kw), cur), pl.BlockSpec((blk, kw), prev),
                  pl.BlockSpec((rows, kw), cur), pl.BlockSpec((blk, kw), prev),
                  pl.BlockSpec(memory_space=pltpu.SMEM)],
        out_specs=pl.BlockSpec((rows, qw), cur),
        out_shape=jax.ShapeDtypeStruct((n, qw), BF16),
        compiler_params=_cparams("parallel", "parallel"),
        name="swa",
    )(qa, ka, ka, va, va, sinks.astype(F32))


def _sb_kernel(q_ref, k_ref, v_ref, tri_ref, o_ref):
    i = pl.program_id(2)
    qb, width = q_ref.shape
    n_heads = 2 * (width // LANES)
    lane = lax.broadcasted_iota(jnp.int32, (qb, LANES), 1)
    low = lane < HEAD_DIM
    scale = jnp.asarray(HEAD_DIM ** -0.5, q_ref.dtype)
    q_heads = []
    for h in range(n_heads):
        q_pair = q_ref[:, (h // 2) * LANES:(h // 2 + 1) * LANES]
        q_heads.append(jnp.where(low if h % 2 == 0 else ~low, q_pair, jnp.zeros_like(q_pair)) * scale)
    tri = tri_ref[...]
    strict = (lax.broadcasted_iota(jnp.int32, (qb, qb), 1)
              < lax.broadcasted_iota(jnp.int32, (qb, qb), 0))

    def step(chunks, carries, accs):
        carries, accs = list(carries), list(accs)
        jobs = [(ci, h) for ci in range(len(chunks)) for h in range(n_heads)]
        z, log_beta, fail, fail_sum, later, w = {}, {}, {}, {}, {}, {}

        def cols(ref, ci, h):
            start = pl.multiple_of(chunks[ci][0] * qb, qb)
            return ref[pl.ds(start, qb), (h // 2) * LANES:(h // 2 + 1) * LANES]

        def scores(j):
            ci, h = jobs[j]
            z[j] = _dot_nt(q_heads[h], cols(k_ref, ci, h))

        def softplus(j):
            sp = jnp.maximum(z[j], 0.0) + jnp.log(1.0 + jnp.exp(-jnp.abs(z[j])))
            log_beta[j] = z[j] - sp
            spm = jnp.where(strict, sp, 0.0) if chunks[jobs[j][0]][1] else sp
            fail[j] = spm.astype(BF16)
            fail_sum[j] = jnp.sum(spm, axis=-1, keepdims=True)

        def suffix_sums(j):
            later[j] = _dot(fail[j], tri)

        def weights(j):
            ci, h = jobs[j]
            wj = jnp.exp(log_beta[j] - (later[j] + carries[h]))
            w[j] = (jnp.where(strict, wj, 0.0) if chunks[ci][1] else wj).astype(BF16)
            carries[h] = carries[h] + fail_sum[j]

        def values(j):
            ci, h = jobs[j]
            accs[h] = accs[h] + _dot(w[j], cols(v_ref, ci, h))

        for stage in (scores, softplus, suffix_sums, weights, values):
            for j in range(len(jobs)):
                stage(j)
        return tuple(carries), tuple(accs)

    carries = (jnp.zeros((qb, 1), F32),) * n_heads
    accs = (jnp.zeros((qb, LANES), F32),) * n_heads
    carries, accs = lax.cond(i == 0,
                             lambda: step([(i, True)], carries, accs),
                             lambda: step([(i, True), (i - 1, False)], carries, accs))

    def cond(state):
        t, carries, _ = state
        lowest = functools.reduce(jnp.minimum, [jnp.min(c) for c in carries])
        return (t < i - 1) & (lowest <= SB_UNDERFLOW)

    def body(state):
        t, carries, accs = state
        carries, accs = step([(i - 2 - t, False)], carries, accs)
        return t + 1, carries, accs

    _, _, accs = lax.while_loop(cond, body, (jnp.int32(0), carries, accs))
    for p in range(width // LANES):
        o_ref[:, p * LANES:(p + 1) * LANES] = jnp.where(low, accs[2 * p], accs[2 * p + 1]).astype(o_ref.dtype)


def _stick_breaking(qb, kb, vb, batch, seq):
    n, w = qb.shape
    blk = min(SB_QBLK, seq)
    nq = seq // blk
    cols = SB_PAIRS_PER_STEP * LANES
    tri = jnp.asarray(np.tril(np.ones((blk, blk), np.float32), -1), BF16)
    return pl.pallas_call(
        _sb_kernel,
        grid=(batch, w // cols, nq),
        in_specs=[pl.BlockSpec((blk, cols), lambda b, p, i: (b * nq + i, p)),
                  pl.BlockSpec((seq, cols), lambda b, p, i: (b, p)),
                  pl.BlockSpec((seq, cols), lambda b, p, i: (b, p)),
                  _resident((blk, blk))],
        out_specs=pl.BlockSpec((blk, cols), lambda b, p, i: (b * nq + i, p)),
        out_shape=jax.ShapeDtypeStruct((n, w), BF16),
        compiler_params=_cparams("parallel", "parallel", "arbitrary"),
        name="stick_breaking",
    )(qb, kb, vb, tri)


def _gelu_tanh(x):
    return 0.5 * x * (1.0 + jnp.tanh(np.sqrt(2.0 / np.pi) * (x + 0.044715 * (x * x * x))))


def _lru_kernel(x_ref, g_ref, cw_ref, cb_ref, wri_ref, bri_ref, lam_ref, o_ref, tail_ref, h_ref):
    j = pl.program_id(1)
    ts, width = x_ref.shape

    @pl.when(j == 0)
    def _():
        tail_ref[...] = jnp.zeros_like(tail_ref)
        h_ref[...] = jnp.zeros_like(h_ref)

    x = x_ref[...].astype(F32)
    tail = tail_ref[...]
    row8 = lax.broadcasted_iota(jnp.int32, (SUBLANES, width), 0)
    y = x * cw_ref[CONV_W - 1:CONV_W, :] + cb_ref[...]
    for d in range(1, CONV_W):
        xs = pltpu.roll(x, d, 0)
        top = jnp.where(row8 < d, pltpu.roll(tail, d, 0), xs[:SUBLANES])
        xs = jnp.concatenate([top, xs[SUBLANES:]], axis=0)
        y = y + xs * cw_ref[CONV_W - 1 - d:CONV_W - d, :]
    tail_ref[...] = x[ts - SUBLANES:]

    ri = _dot(y.astype(BF16), wri_ref[...]) + bri_ref[...]
    r = jax.nn.sigmoid(ri[:, :width])
    gate_i = jax.nn.sigmoid(ri[:, width:])
    lam = lam_ref[...]
    softplus_neg_lam = jnp.maximum(-lam, 0.0) + jnp.log(1.0 + jnp.exp(-jnp.abs(lam)))
    a = jnp.exp2(r * ((-LRU_C * np.log2(np.e)) * softplus_neg_lam))
    u = jnp.sqrt(1.0 - a * a) * (gate_i * y)

    grouped = (ts // SUBLANES, SUBLANES, width)
    a, u = a.reshape(grouped), u.reshape(grouped)
    in_group = lax.broadcasted_iota(jnp.int32, grouped, 1)
    d = 1
    while d < SUBLANES:
        keep = in_group >= d
        a_sh = jnp.where(keep, pltpu.roll(a, d, 1), 1.0)
        u_sh = jnp.where(keep, pltpu.roll(u, d, 1), 0.0)
        u = a * u_sh + u
        a = a * a_sh
        d *= 2
    a, u = a.reshape(ts, width), u.reshape(ts, width)
    state = h_ref[...]
    groups = []
    for g in range(ts // SUBLANES):
        rows = slice(g * SUBLANES, (g + 1) * SUBLANES)
        hg = u[rows] + a[rows] * state
        groups.append(hg)
        state = hg[SUBLANES - 1:SUBLANES, :]
    h = jnp.concatenate(groups, axis=0)
    h_ref[...] = state
    o_ref[...] = (h * _gelu_tanh(g_ref[...].astype(F32))).astype(o_ref.dtype)


def _block_diag(w):
    heads, blk, _ = w.shape
    eye = jnp.eye(heads, dtype=w.dtype)
    return jnp.einsum('hij,hg->higj', w, eye).reshape(heads * blk, heads * blk)


def _rglru(xc, gc, conv_w, conv_b, w_r, b_r, w_i, b_i, lam, batch, seq):
    n, width = xc.shape
    ts = min(LRU_CHUNK, seq)
    nc = seq // ts
    wri = jnp.concatenate([_block_diag(w_r), _block_diag(w_i)], axis=1).astype(BF16)
    bri = jnp.concatenate([b_r, b_i]).astype(F32).reshape(1, 2 * width)
    blk = lambda b, j: (b * nc + j, 0)
    return pl.pallas_call(
        _lru_kernel,
        grid=(batch, nc),
        in_specs=[pl.BlockSpec((ts, width), blk), pl.BlockSpec((ts, width), blk),
                  _resident((CONV_W, width)), _resident((1, width)),
                  _resident((width, 2 * width)), _resident((1, 2 * width)),
                  _resident((1, width))],
        out_specs=pl.BlockSpec((ts, width), blk),
        out_shape=jax.ShapeDtypeStruct((n, width), BF16),
        scratch_shapes=[pltpu.VMEM((SUBLANES, width), F32), pltpu.VMEM((1, width), F32)],
        compiler_params=_cparams("parallel", "arbitrary"),
        name="rglru",
    )(xc, gc, conv_w.astype(F32), conv_b.astype(F32).reshape(1, width), wri, bri,
      lam.astype(F32).reshape(1, width))


def _merge_kernel(x_ref, oa_ref, ob_ref, oc_ref, gt_ref, bg_ref, wa_ref, wb_ref, wc_ref, wo_ref, o_ref):
    tm, d = x_ref.shape
    branches = ((oa_ref, wa_ref), (ob_ref, wb_ref), (oc_ref, wc_ref))
    halves = [slice(r * (tm // 2), (r + 1) * (tm // 2)) for r in range(2)]
    proj = [[_dot(b_ref[rows, :], w_ref[...]) for b_ref, w_ref in branches] for rows in halves]
    for rows, terms in zip(halves, proj):
        merged = None
        for idx, term in enumerate(terms):
            cols = slice(idx * d, (idx + 1) * d)
            gate = jax.nn.sigmoid(gt_ref[rows, cols].astype(F32) + bg_ref[:, cols])
            merged = gate * term if merged is None else merged + gate * term
        o_ref[rows, :] = x_ref[rows, :] + _dot(merged.astype(BF16), wo_ref[...])


def _merge(x2, o_a, o_b, o_c, gates, b_gate, wa, wb, wc, wo):
    n, d = x2.shape
    tm = min(ROW_TILE, n)
    row = lambda i: (i, 0)
    return pl.pallas_call(
        _merge_kernel,
        grid=(n // tm,),
        in_specs=[pl.BlockSpec((tm, d), row),
                  pl.BlockSpec((tm, o_a.shape[1]), row), pl.BlockSpec((tm, o_b.shape[1]), row),
                  pl.BlockSpec((tm, o_c.shape[1]), row), pl.BlockSpec((tm, gates.shape[1]), row),
                  _resident((1, gates.shape[1])),
                  _resident(wa.shape), _resident(wb.shape), _resident(wc.shape), _resident(wo.shape)],
        out_specs=pl.BlockSpec((tm, d), row),
        out_shape=jax.ShapeDtypeStruct((n, d), F32),
        compiler_params=_cparams("parallel"),
        name="merge",
    )(x2, o_a, o_b, o_c, gates, b_gate.astype(F32).reshape(1, -1), wa, wb, wc, wo)


def _swiglu_tile(h, wg_ref, wu_ref, wd_ref, ff_chunk):
    d_ff = wg_ref.shape[-1]
    starts = list(range(0, d_ff, ff_chunk))

    def gate_up(c):
        return _dot(h, wg_ref[:, c:c + ff_chunk]), _dot(h, wu_ref[:, c:c + ff_chunk])

    acc = None
    pending = gate_up(starts[0])
    for i, c in enumerate(starts):
        g, u = pending
        if i + 1 < len(starts):
            pending = gate_up(starts[i + 1])
        act = (g * jax.nn.sigmoid(g) * u).astype(BF16)
        part = _dot(act, wd_ref[c:c + ff_chunk, :])
        acc = part if acc is None else acc + part
    return acc


def _ffn_chunk(d_ff):
    for c in (512, 896, 256, 128):
        if d_ff % c == 0:
            return c
    return d_ff


def _dense_ffn_kernel(ff_chunk, n_side, x_ref, g_ref, wg_ref, wu_ref, wd_ref, *refs):
    side_in, o_ref, side_out = refs[:n_side], refs[n_side], refs[n_side + 1:]
    x = x_ref[...]
    h = _rms(x, g_ref[...]).astype(BF16)
    o_ref[...] = x + _swiglu_tile(h, wg_ref, wu_ref, wd_ref, ff_chunk)
    for src, dst in zip(side_in, side_out):
        dst[...] = src[...].astype(dst.dtype)


def _dense_ffn(x2, gain, wg, wu, wd, side_casts=()):
    n, d = x2.shape
    tm = min(ROW_TILE, n)
    steps = n // tm
    side_in, side_specs_in, side_specs_out, side_shapes, side_final = [], [], [], [], []
    for w, layer in side_casts:
        shape = w.shape[1:]
        cols = shape[-1]
        rows = int(np.prod(shape[:-1]))
        assert rows % (steps * 2 * SUBLANES) == 0
        tr = rows // steps
        side_in.append(w.reshape(w.shape[0], rows, cols))
        side_specs_in.append(pl.BlockSpec((None, tr, cols), lambda i, layer=layer: (layer, i, 0)))
        side_specs_out.append(pl.BlockSpec((tr, cols), lambda i: (i, 0)))
        side_shapes.append(jax.ShapeDtypeStruct((rows, cols), BF16))
        side_final.append(shape)
    outs = pl.pallas_call(
        functools.partial(_dense_ffn_kernel, _ffn_chunk(wg.shape[1]), len(side_casts)),
        grid=(steps,),
        in_specs=[pl.BlockSpec((tm, d), lambda i: (i, 0)), _resident((1, d)),
                  _resident(wg.shape), _resident(wu.shape), _resident(wd.shape)] + side_specs_in,
        out_specs=[pl.BlockSpec((tm, d), lambda i: (i, 0))] + side_specs_out,
        out_shape=[jax.ShapeDtypeStruct((n, d), F32)] + side_shapes,
        compiler_params=_cparams("parallel"),
        name="dense_ffn",
    )(x2, gain, wg, wu, wd, *side_in)
    return outs[0], [o.reshape(s) for o, s in zip(outs[1:], side_final)]


def _router_kernel(n_experts, stride, x_ref, g_ref, wr_ref, tril_ref, h_ref, idx_ref, wt_ref, total_ref, count_ref):
    h = _rms(x_ref[...], g_ref[...])
    h_ref[...] = h.astype(h_ref.dtype)
    w = wr_ref[...]
    h1 = h.astype(BF16)
    h2 = (h - h1.astype(F32)).astype(BF16)
    h3 = (h - h1.astype(F32) - h2.astype(F32)).astype(BF16)
    a, b, c = _dot(h1, w), _dot(h2, w), _dot(h3, w)

    def term(x, k):
        return x if k == 0 else pltpu.roll(x, LANES - k * stride, 1)

    logits = ((term(a, 0) + (term(a, 1) + term(b, 0)))
              + (term(b, 1) + term(a, 2) + term(c, 0)))
    lane = lax.broadcasted_iota(jnp.int32, logits.shape, 1)
    logits = jnp.where(lane < n_experts, logits, NEG_BIG)
    m1 = jnp.max(logits, axis=-1, keepdims=True)
    i1 = jnp.min(jnp.where(logits == m1, lane, LANES), axis=-1, keepdims=True)
    rest = jnp.where(lane == i1, NEG_BIG, logits)
    m2 = jnp.max(rest, axis=-1, keepdims=True)
    i2 = jnp.min(jnp.where(rest == m2, lane, LANES), axis=-1, keepdims=True)
    e2 = jnp.exp(m2 - m1)
    wt1 = 1.0 / (1.0 + e2)
    wt2 = e2 / (1.0 + e2)
    wt_ref[...] = jnp.where(lane == 0, wt1, jnp.where(lane == 1, wt2, 0.0))

    @pl.when(pl.program_id(0) == 0)
    def _():
        count_ref[...] = jnp.zeros_like(count_ref)

    chosen = jnp.where((lane == i1) | (lane == i2), 1.0, 0.0)
    before = _dot(tril_ref[...], chosen.astype(BF16)) + count_ref[...]
    r1 = jnp.sum(jnp.where(lane == i1, before, 0.0), axis=-1, keepdims=True).astype(jnp.int32)
    r2 = jnp.sum(jnp.where(lane == i2, before, 0.0), axis=-1, keepdims=True).astype(jnp.int32)
    count_ref[...] = count_ref[...] + jnp.sum(chosen, axis=0, keepdims=True)
    total_ref[...] = count_ref[...]
    route = jnp.where(lane == 0, i1, jnp.where(lane == 1, i2,
                      jnp.where(lane == 2, r1, jnp.where(lane == 3, r2, 0))))
    idx_ref[...] = jnp.transpose(route)[:idx_ref.shape[0], :]


def _router(x2, gain, w_router):
    n, d = x2.shape
    n_experts = w_router.shape[1]
    tm = min(ROW_TILE, n)
    stride = SUBLANES * (-(-n_experts // SUBLANES))
    assert 3 * stride <= LANES
    w = w_router.astype(F32)
    w1 = w.astype(BF16)
    w2 = (w - w1.astype(F32)).astype(BF16)
    w3 = (w - w1.astype(F32) - w2.astype(F32)).astype(BF16)
    wr = jnp.zeros((d, LANES), BF16)
    for k, wk in enumerate((w1, w2, w3)):
        wr = wr.at[:, k * stride:k * stride + n_experts].set(wk)
    tril = jnp.asarray(np.tril(np.ones((tm, tm), np.float32), -1), BF16)
    row = lambda i: (i, 0)
    return pl.pallas_call(
        functools.partial(_router_kernel, n_experts, stride),
        grid=(n // tm,),
        in_specs=[pl.BlockSpec((tm, d), row), _resident((1, d)), _resident((d, LANES)), _resident((tm, tm))],
        out_specs=[pl.BlockSpec((tm, d), row), pl.BlockSpec((SUBLANES, tm), lambda i: (0, i)),
                   pl.BlockSpec((tm, LANES), row), pl.BlockSpec((1, LANES), lambda i: (0, 0))],
        out_shape=[jax.ShapeDtypeStruct((n, d), BF16), jax.ShapeDtypeStruct((SUBLANES, n), jnp.int32),
                   jax.ShapeDtypeStruct((n, LANES), F32), jax.ShapeDtypeStruct((1, LANES), F32)],
        scratch_shapes=[pltpu.VMEM((1, LANES), F32)],
        compiler_params=_cparams("arbitrary"),
        name="router",
    )(x2, gain, wr, tril)


def _moe_kernel(ff_chunk, te_ref, tv_ref, x_ref, wg_ref, wu_ref, wd_ref, o_ref):
    t = pl.program_id(0)

    @pl.when(tv_ref[t] > 0)
    def _():
        y = _swiglu_tile(x_ref[...], wg_ref.at[0], wu_ref.at[0], wd_ref.at[0], ff_chunk)
        o_ref[...] = y.astype(o_ref.dtype)

    @pl.when(tv_ref[t] == 0)
    def _():
        o_ref[...] = jnp.zeros_like(o_ref)


def _moe_experts(xs, tile_expert, tile_valid, wg, wu, wd):
    p, d = xs.shape
    d_ff = wg.shape[2]
    tm = MOE_TILE
    grid_spec = pltpu.PrefetchScalarGridSpec(
        num_scalar_prefetch=2,
        grid=(p // tm,),
        in_specs=[pl.BlockSpec((tm, d), lambda t, te, tv: (t, 0)),
                  pl.BlockSpec((1, d, d_ff), lambda t, te, tv: (te[t], 0, 0)),
                  pl.BlockSpec((1, d, d_ff), lambda t, te, tv: (te[t], 0, 0)),
                  pl.BlockSpec((1, d_ff, d), lambda t, te, tv: (te[t], 0, 0))],
        out_specs=pl.BlockSpec((tm, d), lambda t, te, tv: (t, 0)),
    )
    return pl.pallas_call(
        functools.partial(_moe_kernel, _ffn_chunk(d_ff)),
        grid_spec=grid_spec,
        out_shape=jax.ShapeDtypeStruct((p, d), BF16),
        compiler_params=_cparams("arbitrary"),
        name="moe_experts",
    )(tile_expert, tile_valid, xs, wg, wu, wd)


def _combine_kernel(x_ref, wt_ref, y0_ref, y1_ref, o_ref):
    wt = wt_ref[...]
    o_ref[...] = x_ref[...] + (wt[:, 0:1] * y0_ref[...].astype(F32) + wt[:, 1:2] * y1_ref[...].astype(F32))


def _combine(x2, wt, y0, y1):
    n, d = x2.shape
    tm = min(ROW_TILE, n)
    row = lambda i: (i, 0)
    return pl.pallas_call(
        _combine_kernel,
        grid=(n // tm,),
        in_specs=[pl.BlockSpec((tm, d), row), pl.BlockSpec((tm, LANES), row),
                  pl.BlockSpec((tm, d), row), pl.BlockSpec((tm, d), row)],
        out_specs=pl.BlockSpec((tm, d), row),
        out_shape=jax.ShapeDtypeStruct((n, d), F32),
        compiler_params=_cparams("parallel"),
        name="moe_combine",
    )(x2, wt, y0, y1)


def _moe_ffn(x2, gain, w_router, expert_weights):
    n, d = x2.shape
    n_experts = w_router.shape[1]
    tm = MOE_TILE
    h, route, top_w, totals = _router(x2, gain, w_router)

    sizes = totals[0, :n_experts].astype(jnp.int32)
    padded = ((sizes + tm - 1) // tm) * tm
    starts = jnp.cumsum(padded) - padded
    pos = []
    for k in range(TOP_K):
        expert, rank = route[k], route[TOP_K + k]
        base = jnp.zeros_like(rank)
        for e in range(n_experts):
            base = base + jnp.where(expert == e, starts[e], 0)
        pos.append(rank + base)
    n_tiles = (n * TOP_K) // tm + n_experts
    p = n_tiles * tm
    tile_start = jnp.arange(n_tiles, dtype=jnp.int32) * tm
    ends = starts + padded
    tile_expert = jnp.minimum(jnp.sum((tile_start[:, None] >= ends[None, :]).astype(jnp.int32), axis=1),
                              n_experts - 1).astype(jnp.int32)
    tile_valid = (tile_start < ends[-1]).astype(jnp.int32)

    token_bits = max(1, (n - 1).bit_length())
    assert p < (1 << (32 - token_bits))
    token = jnp.tile(jnp.arange(n, dtype=jnp.uint32), TOP_K)
    keys = (jnp.concatenate(pos).astype(jnp.uint32) << token_bits) | token
    compact = (jnp.sort(keys) & ((1 << token_bits) - 1)).astype(jnp.int32)
    max_pad = p - n * TOP_K
    filler = jnp.arange(max_pad, dtype=jnp.int32) % n
    spread = jnp.concatenate([filler, compact, filler])
    run_start = jnp.cumsum(sizes) - sizes
    slot_expert = jnp.repeat(tile_expert, tm, total_repeat_length=p)
    row_src = jnp.zeros((p,), jnp.int32)
    for e in range(n_experts):
        shifted = lax.dynamic_slice(spread, (max_pad - (starts[e] - run_start[e]),), (p,))
        row_src = jnp.where(slot_expert == e, shifted, row_src)

    xs = h.at[row_src].get(mode="promise_in_bounds")
    ys = _moe_experts(xs, tile_expert, tile_valid, *expert_weights)
    y0 = ys.at[pos[0]].get(mode="promise_in_bounds")
    y1 = ys.at[pos[1]].get(mode="promise_in_bounds")
    return _combine(x2, top_w, y0, y1)


def kernel(x, attn_norm, w_in, b_gate, q_norm, k_norm, sinks, conv_w, conv_b, lru_w_r, lru_b_r, lru_w_i, lru_b_i, lru_lambda, w_proj_a, w_proj_b, w_proj_c, w_out, ffn_norm, w_ffn_gate, w_ffn_up, w_ffn_down, w_router, w_exp_gate, w_exp_up, w_exp_down):
    batch, seq, d = x.shape
    depth = w_in.shape[0]
    swa_q = w_proj_a.shape[1]
    sb_w = w_proj_b.shape[1]
    lru_w = w_proj_c.shape[1]
    swa_kv = SWA_KV_HEADS * HEAD_DIM
    widths = (swa_q, swa_kv, swa_kv, sb_w, sb_w, sb_w, lru_w, lru_w, N_BRANCHES * d)
    assert sum(widths) == w_in.shape[2]
    assert seq % SB_QBLK == 0 and seq % WINDOW == 0

    x2 = x.reshape(batch * seq, d).astype(F32)
    for l in range(depth):
        gain = attn_norm[l].astype(F32).reshape(1, d)
        qa, ka, va, qb, kb, vb, xc, gc, gates = _in_proj(x2, gain, _to_bf16(w_in, l), widths,
                                                         q_norm[l], k_norm[l])
        o_a = _swa(qa, ka, va, sinks[l], batch, seq)
        o_b = _stick_breaking(qb, kb, vb, batch, seq)
        o_c = _rglru(xc, gc, conv_w[l], conv_b[l], lru_w_r[l], lru_b_r[l], lru_w_i[l], lru_b_i[l],
                     lru_lambda[l], batch, seq)
        x2 = _merge(x2, o_a, o_b, o_c, gates, b_gate[l],
                    _to_bf16(w_proj_a, l), _to_bf16(w_proj_b, l), _to_bf16(w_proj_c, l),
                    _to_bf16(w_out, l))
        fgain = ffn_norm[l].astype(F32).reshape(1, d)
        j = l // 2
        if l % 2 == 0:
            side = [(w, j) for w in (w_exp_gate, w_exp_up, w_exp_down)] if l + 1 < depth else []
            x2, expert_bf16 = _dense_ffn(x2, fgain, _to_bf16(w_ffn_gate, j), _to_bf16(w_ffn_up, j),
                                         _to_bf16(w_ffn_down, j), side)
        else:
            x2 = _moe_ffn(x2, fgain, w_router[j], expert_bf16)
    return x2.reshape(batch, seq, d).astype(x.dtype)
```

```python
import functools

import numpy as np
import jax
import jax.numpy as jnp
from jax import lax
from jax.experimental import pallas as pl
from jax.experimental.pallas import tpu as pltpu

F32 = jnp.float32
BF16 = jnp.bfloat16

HEAD_DIM = 64
SWA_KV_HEADS = 2
WINDOW = 128
LRU_HEADS = 8
LRU_C = 8.0
CONV_W = 4
N_BRANCHES = 3
TOP_K = 2
EPS = 1e-6

LANES = 128
SUBLANES = 8
VMEM_LIMIT = 56 * 1024 * 1024
NEG_BIG = -1e30

SWA_BLOCKS_PER_STEP = 4
SB_QBLK = 256
SB_PAIRS_PER_STEP = 4
SB_UNDERFLOW = 105.0
LRU_CHUNK = 512
ROW_TILE = 512
MOE_TILE = 512
MERGE_TILE = 1024


def _cparams(*sem):
    return pltpu.CompilerParams(dimension_semantics=sem, vmem_limit_bytes=VMEM_LIMIT)


def _resident(shape):
    nd = len(shape)
    return pl.BlockSpec(shape, lambda *_: (0,) * nd, pipeline_mode=pl.Buffered(1))


def _dot(a, b):
    return jnp.dot(a, b, preferred_element_type=F32)


def _dot_nt(a, b):
    return lax.dot_general(a, b, (((1,), (1,)), ((), ())), preferred_element_type=F32)


def _split_dot(a, b_bf16):
    hi = a.astype(BF16)
    lo = (a - hi.astype(F32)).astype(BF16)
    return _dot(hi, b_bf16) + _dot(lo, b_bf16)


def _rms(x, gain):
    ms = jnp.mean(x * x, axis=-1, keepdims=True)
    return x * lax.rsqrt(ms + EPS) * gain


CAST_BLOCK_BYTES = 4 * 1024 * 1024


def _cast_kernel(x_ref, o_ref):
    o_ref[...] = x_ref[...].astype(o_ref.dtype)


def _to_bf16(w, layer):
    shape = w.shape[1:]
    cols = shape[-1]
    rows = int(np.prod(shape[:-1]))
    tr = rows
    while tr * cols * 4 > CAST_BLOCK_BYTES and tr % 32 == 0:
        tr //= 2
    out = pl.pallas_call(
        _cast_kernel,
        grid=(rows // tr,),
        in_specs=[pl.BlockSpec((None, tr, cols), lambda i: (layer, i, 0))],
        out_specs=pl.BlockSpec((tr, cols), lambda i: (i, 0)),
        out_shape=jax.ShapeDtypeStruct((rows, cols), BF16),
        compiler_params=_cparams("parallel"),
        name="cast_bf16",
    )(w.reshape(w.shape[0], rows, cols))
    return out.reshape(shape)


def _head_norm(xf, bd, gain):
    ms = _dot((xf * xf).astype(BF16), bd)
    return xf * lax.rsqrt(ms + EPS) * gain


def _in_proj_kernel(widths, x_ref, g_ref, w_ref, bdq_ref, bdk_ref, qg_ref, kg_ref, *out_refs):
    h = _rms(x_ref[...], g_ref[...]).astype(BF16)
    normed = ((bdq_ref, qg_ref), (bdk_ref, kg_ref))
    offs = [sum(widths[:i]) for i in range(len(widths))]
    accs = [_dot(h, w_ref[:, offs[i]:offs[i] + widths[i]]) for i in range(len(normed))]
    for idx in range(len(normed), len(widths)):
        ref, width, off = out_refs[idx], widths[idx], offs[idx]
        for c in range(0, width, 1024):
            cw = min(1024, width - c)
            ref[:, c:c + cw] = _dot(h, w_ref[:, off + c:off + c + cw]).astype(ref.dtype)
    for idx, (bd_ref, gain_ref) in enumerate(normed):
        out_refs[idx][...] = _head_norm(accs[idx], bd_ref[...], gain_ref[...]).astype(out_refs[idx].dtype)


def _block_diag_mean(width):
    idx = np.arange(width) // HEAD_DIM
    return jnp.asarray((idx[:, None] == idx[None, :]).astype(np.float32) / HEAD_DIM, BF16)


def _in_proj(x2, gain, w_bf16, widths, q_norm, k_norm):
    n, d = x2.shape
    tm = min(ROW_TILE, n)
    qw, kw = widths[0], widths[1]
    qg = (jnp.tile(q_norm.astype(F32), qw // HEAD_DIM) * (HEAD_DIM ** -0.5)).reshape(1, qw)
    kg = jnp.tile(k_norm.astype(F32), kw // HEAD_DIM).reshape(1, kw)
    return pl.pallas_call(
        functools.partial(_in_proj_kernel, widths),
        grid=(n // tm,),
        in_specs=[pl.BlockSpec((tm, d), lambda i: (i, 0)),
                  _resident((1, d)),
                  _resident(w_bf16.shape),
                  _resident((qw, qw)), _resident((kw, kw)),
                  _resident((1, qw)), _resident((1, kw))],
        out_specs=[pl.BlockSpec((tm, w), lambda i: (i, 0)) for w in widths],
        out_shape=[jax.ShapeDtypeStruct((n, w), BF16) for w in widths],
        compiler_params=_cparams("parallel"),
        name="in_proj",
    )(x2, gain, w_bf16, _block_diag_mean(qw), _block_diag_mean(kw), qg, kg)


def _swa_kernel(n_heads, blk, q_ref, kc_ref, kp_ref, vc_ref, vp_ref, sink_ref, o_ref):
    n = pl.program_id(1)
    n_blocks = q_ref.shape[0] // blk
    group = n_heads // SWA_KV_HEADS

    upper = (lax.broadcasted_iota(jnp.int32, (blk, blk), 1)
             > lax.broadcasted_iota(jnp.int32, (blk, blk), 0))
    no_prev = upper & (n == 0)
    lane = lax.broadcasted_iota(jnp.int32, (blk, LANES), 1)
    low = lane < HEAD_DIM

    def swapped(t):
        return (t, pltpu.roll(t, HEAD_DIM, 1))

    rows = [slice(b * blk, (b + 1) * blk) for b in range(n_blocks)]
    k_blocks = [swapped(kp_ref[...])] + [swapped(kc_ref[r, :]) for r in rows]
    v_blocks = [swapped(vp_ref[...])] + [swapped(vc_ref[r, :]) for r in rows]
    kt_blocks = [tuple(jnp.transpose(t) for t in pair) for pair in k_blocks]

    jobs = [(b, j) for b in range(n_blocks) for j in range(n_heads)]
    sel = [(j // group + j % 2) % 2 for j in range(n_heads)]
    s_prev, s_cur, p_prev, p_cur, denom, outs = {}, {}, {}, {}, {}, {}

    def scores(t):
        b, j = jobs[t]
        q_pair = q_ref[rows[b], (j // 2) * LANES:(j // 2 + 1) * LANES]
        qm = jnp.where(low if j % 2 == 0 else ~low, q_pair, jnp.zeros_like(q_pair))
        s_prev[t] = _dot(qm, kt_blocks[b][sel[j]])
        s_cur[t] = _dot(qm, kt_blocks[b + 1][sel[j]])

    def softmax(t):
        b, j = jobs[t]
        s = jnp.where(upper, s_prev[t], s_cur[t])
        if b == 0:
            s = jnp.where(no_prev, NEG_BIG, s)
        sink = sink_ref[j]
        m = jnp.maximum(jnp.max(s, axis=-1, keepdims=True), sink)
        p = jnp.exp(s - m)
        denom[t] = jnp.sum(p, axis=-1, keepdims=True) + jnp.exp(sink - m)
        p_prev[t] = jnp.where(upper, p, 0.0).astype(BF16)
        p_cur[t] = jnp.where(upper, 0.0, p).astype(BF16)

    def values(t):
        b, j = jobs[t]
        outs[t] = (_dot(p_prev[t], v_blocks[b][sel[j]]), _dot(p_cur[t], v_blocks[b + 1][sel[j]]))

    def finish(t):
        b, j = jobs[t]
        outs[t] = (outs[t][0] + outs[t][1]) / denom[t]
        if j % 2 == 1:
            o_ref[rows[b], (j // 2) * LANES:(j // 2 + 1) * LANES] = jnp.where(
                low, outs[t - 1], outs[t]).astype(o_ref.dtype)

    for stage in (scores, softmax, values, finish):
        for t in range(len(jobs)):
            stage(t)


def _swa(qa, ka, va, sinks, batch, seq):
    n, qw = qa.shape
    kw = ka.shape[1]
    n_heads = qw // HEAD_DIM
    blk = WINDOW
    per_step = min(SWA_BLOCKS_PER_STEP, seq // blk)
    assert seq % (per_step * blk) == 0
    rows = per_step * blk
    ns = seq // rows
    cur = lambda b, i: (b * ns + i, 0)
    prev = lambda b, i: (b * ns * per_step + jnp.maximum(i * per_step - 1, 0), 0)
    return pl.pallas_call(
        functools.partial(_swa_kernel, n_heads, blk),
        grid=(batch, ns),
        in_specs=[pl.BlockSpec((rows, qw), cur),
                  pl.BlockSpec((rows, kw), cur), pl.BlockSpec((blk, kw), prev),
                  pl.BlockSpec((rows, kw), cur), pl.BlockSpec((blk, kw), prev),
                  pl.BlockSpec(memory_space=pltpu.SMEM)],
        out_specs=pl.BlockSpec((rows, qw), cur),
        out_shape=jax.ShapeDtypeStruct((n, qw), BF16),
        compiler_params=_cparams("parallel", "parallel"),
        name="swa",
    )(qa, ka, ka, va, va, sinks.astype(F32))


def _sb_kernel(q_ref, k_ref, v_ref, tri_ref, o_ref):
    i = pl.program_id(2)
    qb, width = q_ref.shape
    n_heads = 2 * (width // LANES)
    lane = lax.broadcasted_iota(jnp.int32, (qb, LANES), 1)
    low = lane < HEAD_DIM
    scale = jnp.asarray(HEAD_DIM ** -0.5, q_ref.dtype)
    q_heads = []
    for h in range(n_heads):
        q_pair = q_ref[:, (h // 2) * LANES:(h // 2 + 1) * LANES]
        q_heads.append(jnp.where(low if h % 2 == 0 else ~low, q_pair, jnp.zeros_like(q_pair)) * scale)
    tri = tri_ref[...]
    strict = (lax.broadcasted_iota(jnp.int32, (qb, qb), 1)
              < lax.broadcasted_iota(jnp.int32, (qb, qb), 0))

    def step(chunks, carries, accs):
        carries, accs = list(carries), list(accs)
        jobs = [(ci, h) for ci in range(len(chunks)) for h in range(n_heads)]
        z, log_beta, fail, fail_sum, later, w = {}, {}, {}, {}, {}, {}

        def cols(ref, ci, h):
            start = pl.multiple_of(chunks[ci][0] * qb, qb)
            return ref[pl.ds(start, qb), (h // 2) * LANES:(h // 2 + 1) * LANES]

        def scores(j):
            ci, h = jobs[j]
            z[j] = _dot_nt(q_heads[h], cols(k_ref, ci, h))

        def softplus(j):
            sp = jnp.maximum(z[j], 0.0) + jnp.log(1.0 + jnp.exp(-jnp.abs(z[j])))
            log_beta[j] = z[j] - sp
            spm = jnp.where(strict, sp, 0.0) if chunks[jobs[j][0]][1] else sp
            fail[j] = spm.astype(BF16)
            fail_sum[j] = jnp.sum(spm, axis=-1, keepdims=True)

        def suffix_sums(j):
            later[j] = _dot(fail[j], tri)

        def weights(j):
            ci, h = jobs[j]
            wj = jnp.exp(log_beta[j] - (later[j] + carries[h]))
            w[j] = (jnp.where(strict, wj, 0.0) if chunks[ci][1] else wj).astype(BF16)
            carries[h] = carries[h] + fail_sum[j]

        def values(j):
            ci, h = jobs[j]
            accs[h] = accs[h] + _dot(w[j], cols(v_ref, ci, h))

        for stage in (scores, softplus, suffix_sums, weights, values):
            for j in range(len(jobs)):
                stage(j)
        return tuple(carries), tuple(accs)

    carries = (jnp.zeros((qb, 1), F32),) * n_heads
    accs = (jnp.zeros((qb, LANES), F32),) * n_heads
    carries, accs = lax.cond(i == 0,
                             lambda: step([(i, True)], carries, accs),
                             lambda: step([(i, True), (i - 1, False)], carries, accs))

    def cond(state):
        t, carries, _ = state
        lowest = functools.reduce(jnp.minimum, [jnp.min(c) for c in carries])
        return (t < i - 1) & (lowest <= SB_UNDERFLOW)

    def body(state):
        t, carries, accs = state
        carries, accs = step([(i - 2 - t, False)], carries, accs)
        return t + 1, carries, accs

    _, _, accs = lax.while_loop(cond, body, (jnp.int32(0), carries, accs))
    for p in range(width // LANES):
        o_ref[:, p * LANES:(p + 1) * LANES] = jnp.where(low, accs[2 * p], accs[2 * p + 1]).astype(o_ref.dtype)


def _stick_breaking(qb, kb, vb, batch, seq):
    n, w = qb.shape
    blk = min(SB_QBLK, seq)
    nq = seq // blk
    cols = SB_PAIRS_PER_STEP * LANES
    tri = jnp.asarray(np.tril(np.ones((blk, blk), np.float32), -1), BF16)
    return pl.pallas_call(
        _sb_kernel,
        grid=(batch, w // cols, nq),
        in_specs=[pl.BlockSpec((blk, cols), lambda b, p, i: (b * nq + i, p)),
                  pl.BlockSpec((seq, cols), lambda b, p, i: (b, p)),
                  pl.BlockSpec((seq, cols), lambda b, p, i: (b, p)),
                  _resident((blk, blk))],
        out_specs=pl.BlockSpec((blk, cols), lambda b, p, i: (b * nq + i, p)),
        out_shape=jax.ShapeDtypeStruct((n, w), BF16),
        compiler_params=_cparams("parallel", "parallel", "arbitrary"),
        name="stick_breaking",
    )(qb, kb, vb, tri)


def _gelu_tanh(x):
    return 0.5 * x * (1.0 + jnp.tanh(np.sqrt(2.0 / np.pi) * (x + 0.044715 * (x * x * x))))


def _lru_kernel(x_ref, g_ref, cw_ref, cb_ref, wri_ref, bri_ref, lam_ref, o_ref, tail_ref, h_ref):
    j = pl.program_id(1)
    ts, width = x_ref.shape

    @pl.when(j == 0)
    def _():
        tail_ref[...] = jnp.zeros_like(tail_ref)
        h_ref[...] = jnp.zeros_like(h_ref)

    x = x_ref[...].astype(F32)
    tail = tail_ref[...]
    row8 = lax.broadcasted_iota(jnp.int32, (SUBLANES, width), 0)
    y = x * cw_ref[CONV_W - 1:CONV_W, :] + cb_ref[...]
    for d in range(1, CONV_W):
        xs = pltpu.roll(x, d, 0)
        top = jnp.where(row8 < d, pltpu.roll(tail, d, 0), xs[:SUBLANES])
        xs = jnp.concatenate([top, xs[SUBLANES:]], axis=0)
        y = y + xs * cw_ref[CONV_W - 1 - d:CONV_W - d, :]
    tail_ref[...] = x[ts - SUBLANES:]

    ri = _dot(y.astype(BF16), wri_ref[...]) + bri_ref[...]
    r = jax.nn.sigmoid(ri[:, :width])
    gate_i = jax.nn.sigmoid(ri[:, width:])
    lam = lam_ref[...]
    softplus_neg_lam = jnp.maximum(-lam, 0.0) + jnp.log(1.0 + jnp.exp(-jnp.abs(lam)))
    a = jnp.exp2(r * ((-LRU_C * np.log2(np.e)) * softplus_neg_lam))
    u = jnp.sqrt(1.0 - a * a) * (gate_i * y)

    grouped = (ts // SUBLANES, SUBLANES, width)
    a, u = a.reshape(grouped), u.reshape(grouped)
    in_group = lax.broadcasted_iota(jnp.int32, grouped, 1)
    d = 1
    while d < SUBLANES:
        keep = in_group >= d
        a_sh = jnp.where(keep, pltpu.roll(a, d, 1), 1.0)
        u_sh = jnp.where(keep, pltpu.roll(u, d, 1), 0.0)
        u = a * u_sh + u
        a = a * a_sh
        d *= 2
    a, u = a.reshape(ts, width), u.reshape(ts, width)
    state = h_ref[...]
    groups = []
    for g in range(ts // SUBLANES):
        rows = slice(g * SUBLANES, (g + 1) * SUBLANES)
        hg = u[rows] + a[rows] * state
        groups.append(hg)
        state = hg[SUBLANES - 1:SUBLANES, :]
    h = jnp.concatenate(groups, axis=0)
    h_ref[...] = state
    o_ref[...] = (h * _gelu_tanh(g_ref[...].astype(F32))).astype(o_ref.dtype)


def _block_diag(w):
    heads, blk, _ = w.shape
    eye = jnp.eye(heads, dtype=w.dtype)
    return jnp.einsum('hij,hg->higj', w, eye).reshape(heads * blk, heads * blk)


def _rglru(xc, gc, conv_w, conv_b, w_r, b_r, w_i, b_i, lam, batch, seq):
    n, width = xc.shape
    ts = min(LRU_CHUNK, seq)
    nc = seq // ts
    wri = jnp.concatenate([_block_diag(w_r), _block_diag(w_i)], axis=1).astype(BF16)
    bri = jnp.concatenate([b_r, b_i]).astype(F32).reshape(1, 2 * width)
    blk = lambda b, j: (b * nc + j, 0)
    return pl.pallas_call(
        _lru_kernel,
        grid=(batch, nc),
        in_specs=[pl.BlockSpec((ts, width), blk), pl.BlockSpec((ts, width), blk),
                  _resident((CONV_W, width)), _resident((1, width)),
                  _resident((width, 2 * width)), _resident((1, 2 * width)),
                  _resident((1, width))],
        out_specs=pl.BlockSpec((ts, width), blk),
        out_shape=jax.ShapeDtypeStruct((n, width), BF16),
        scratch_shapes=[pltpu.VMEM((SUBLANES, width), F32), pltpu.VMEM((1, width), F32)],
        compiler_params=_cparams("parallel", "arbitrary"),
        name="rglru",
    )(xc, gc, conv_w.astype(F32), conv_b.astype(F32).reshape(1, width), wri, bri,
      lam.astype(F32).reshape(1, width))


def _merge_kernel(x_ref, oa_ref, ob_ref, oc_ref, gt_ref, bg_ref, wa_ref, wb_ref, wc_ref, wo_ref, o_ref):
    tm, d = x_ref.shape
    branches = ((oa_ref, wa_ref), (ob_ref, wb_ref), (oc_ref, wc_ref))
    n_groups = max(2, tm // 256)
    halves = [slice(r * (tm // n_groups), (r + 1) * (tm // n_groups)) for r in range(n_groups)]
    proj = [[_dot(b_ref[rows, :], w_ref[...]) for b_ref, w_ref in branches] for rows in halves]
    for rows, terms in zip(halves, proj):
        merged = None
        for idx, term in enumerate(terms):
            cols = slice(idx * d, (idx + 1) * d)
            gate = jax.nn.sigmoid(gt_ref[rows, cols].astype(F32) + bg_ref[:, cols])
            merged = gate * term if merged is None else merged + gate * term
        o_ref[rows, :] = x_ref[rows, :] + _dot(merged.astype(BF16), wo_ref[...])


def _merge(x2, o_a, o_b, o_c, gates, b_gate, wa, wb, wc, wo):
    n, d = x2.shape
    tm = min(MERGE_TILE, n)
    row = lambda i: (i, 0)
    return pl.pallas_call(
        _merge_kernel,
        grid=(n // tm,),
        in_specs=[pl.BlockSpec((tm, d), row),
                  pl.BlockSpec((tm, o_a.shape[1]), row), pl.BlockSpec((tm, o_b.shape[1]), row),
                  pl.BlockSpec((tm, o_c.shape[1]), row), pl.BlockSpec((tm, gates.shape[1]), row),
                  _resident((1, gates.shape[1])),
                  _resident(wa.shape), _resident(wb.shape), _resident(wc.shape), _resident(wo.shape)],
        out_specs=pl.BlockSpec((tm, d), row),
        out_shape=jax.ShapeDtypeStruct((n, d), F32),
        compiler_params=_cparams("parallel"),
        name="merge",
    )(x2, o_a, o_b, o_c, gates, b_gate.astype(F32).reshape(1, -1), wa, wb, wc, wo)


def _swiglu_tile(h, wg_ref, wu_ref, wd_ref, ff_chunk):
    d_ff = wg_ref.shape[-1]
    starts = list(range(0, d_ff, ff_chunk))

    def gate_up(c):
        return _dot(h, wg_ref[:, c:c + ff_chunk]), _dot(h, wu_ref[:, c:c + ff_chunk])

    acc = None
    pending = gate_up(starts[0])
    for i, c in enumerate(starts):
        g, u = pending
        if i + 1 < len(starts):
            pending = gate_up(starts[i + 1])
        act = (g * jax.nn.sigmoid(g) * u).astype(BF16)
        part = _dot(act, wd_ref[c:c + ff_chunk, :])
        acc = part if acc is None else acc + part
    return acc


def _ffn_chunk(d_ff):
    for c in (512, 896, 256, 128):
        if d_ff % c == 0:
            return c
    return d_ff


def _dense_ffn_kernel(ff_chunk, n_side, x_ref, g_ref, wg_ref, wu_ref, wd_ref, *refs):
    side_in, o_ref, side_out = refs[:n_side], refs[n_side], refs[n_side + 1:]
    x = x_ref[...]
    h = _rms(x, g_ref[...]).astype(BF16)
    o_ref[...] = x + _swiglu_tile(h, wg_ref, wu_ref, wd_ref, ff_chunk)
    for src, dst in zip(side_in, side_out):
        dst[...] = src[...].astype(dst.dtype)


def _dense_ffn(x2, gain, wg, wu, wd, side_casts=()):
    n, d = x2.shape
    tm = min(ROW_TILE, n)
    steps = n // tm
    side_in, side_specs_in, side_specs_out, side_shapes, side_final = [], [], [], [], []
    for w, layer in side_casts:
        shape = w.shape[1:]
        cols = shape[-1]
        rows = int(np.prod(shape[:-1]))
        assert rows % (steps * 2 * SUBLANES) == 0
        tr = rows // steps
        side_in.append(w.reshape(w.shape[0], rows, cols))
        side_specs_in.append(pl.BlockSpec((None, tr, cols), lambda i, layer=layer: (layer, i, 0)))
        side_specs_out.append(pl.BlockSpec((tr, cols), lambda i: (i, 0)))
        side_shapes.append(jax.ShapeDtypeStruct((rows, cols), BF16))
        side_final.append(shape)
    outs = pl.pallas_call(
        functools.partial(_dense_ffn_kernel, _ffn_chunk(wg.shape[1]), len(side_casts)),
        grid=(steps,),
        in_specs=[pl.BlockSpec((tm, d), lambda i: (i, 0)), _resident((1, d)),
                  _resident(wg.shape), _resident(wu.shape), _resident(wd.shape)] + side_specs_in,
        out_specs=[pl.BlockSpec((tm, d), lambda i: (i, 0))] + side_specs_out,
        out_shape=[jax.ShapeDtypeStruct((n, d), F32)] + side_shapes,
        compiler_params=_cparams("parallel"),
        name="dense_ffn",
    )(x2, gain, wg, wu, wd, *side_in)
    return outs[0], [o.reshape(s) for o, s in zip(outs[1:], side_final)]


def _router_kernel(n_experts, stride, x_ref, g_ref, wr_ref, tril_ref, h_ref, idx_ref, wt_ref, total_ref, count_ref):
    h = _rms(x_ref[...], g_ref[...])
    h_ref[...] = h.astype(h_ref.dtype)
    w = wr_ref[...]
    h1 = h.astype(BF16)
    h2 = (h - h1.astype(F32)).astype(BF16)
    h3 = (h - h1.astype(F32) - h2.astype(F32)).astype(BF16)
    a, b, c = _dot(h1, w), _dot(h2, w), _dot(h3, w)

    def term(x, k):
        return x if k == 0 else pltpu.roll(x, LANES - k * stride, 1)

    logits = ((term(a, 0) + (term(a, 1) + term(b, 0)))
              + (term(b, 1) + term(a, 2) + term(c, 0)))
    lane = lax.broadcasted_iota(jnp.int32, logits.shape, 1)
    logits = jnp.where(lane < n_experts, logits, NEG_BIG)
    m1 = jnp.max(logits, axis=-1, keepdims=True)
    i1 = jnp.min(jnp.where(logits == m1, lane, LANES), axis=-1, keepdims=True)
    rest = jnp.where(lane == i1, NEG_BIG, logits)
    m2 = jnp.max(rest, axis=-1, keepdims=True)
    i2 = jnp.min(jnp.where(rest == m2, lane, LANES), axis=-1, keepdims=True)
    e2 = jnp.exp(m2 - m1)
    wt1 = 1.0 / (1.0 + e2)
    wt2 = e2 / (1.0 + e2)
    wt_ref[...] = jnp.where(lane == 0, wt1, jnp.where(lane == 1, wt2, 0.0))

    @pl.when(pl.program_id(0) == 0)
    def _():
        count_ref[...] = jnp.zeros_like(count_ref)

    chosen = jnp.where((lane == i1) | (lane == i2), 1.0, 0.0)
    before = _dot(tril_ref[...], chosen.astype(BF16)) + count_ref[...]
    r1 = jnp.sum(jnp.where(lane == i1, before, 0.0), axis=-1, keepdims=True).astype(jnp.int32)
    r2 = jnp.sum(jnp.where(lane == i2, before, 0.0), axis=-1, keepdims=True).astype(jnp.int32)
    count_ref[...] = count_ref[...] + jnp.sum(chosen, axis=0, keepdims=True)
    total_ref[...] = count_ref[...]
    route = jnp.where(lane == 0, i1, jnp.where(lane == 1, i2,
                      jnp.where(lane == 2, r1, jnp.where(lane == 3, r2, 0))))
    idx_ref[...] = jnp.transpose(route)[:idx_ref.shape[0], :]


def _router(x2, gain, w_router):
    n, d = x2.shape
    n_experts = w_router.shape[1]
    tm = min(ROW_TILE, n)
    stride = SUBLANES * (-(-n_experts // SUBLANES))
    assert 3 * stride <= LANES
    w = w_router.astype(F32)
    w1 = w.astype(BF16)
    w2 = (w - w1.astype(F32)).astype(BF16)
    w3 = (w - w1.astype(F32) - w2.astype(F32)).astype(BF16)
    wr = jnp.zeros((d, LANES), BF16)
    for k, wk in enumerate((w1, w2, w3)):
        wr = wr.at[:, k * stride:k * stride + n_experts].set(wk)
    tril = jnp.asarray(np.tril(np.ones((tm, tm), np.float32), -1), BF16)
    row = lambda i: (i, 0)
    return pl.pallas_call(
        functools.partial(_router_kernel, n_experts, stride),
        grid=(n // tm,),
        in_specs=[pl.BlockSpec((tm, d), row), _resident((1, d)), _resident((d, LANES)), _resident((tm, tm))],
        out_specs=[pl.BlockSpec((tm, d), row), pl.BlockSpec((SUBLANES, tm), lambda i: (0, i)),
                   pl.BlockSpec((tm, LANES), row), pl.BlockSpec((1, LANES), lambda i: (0, 0))],
        out_shape=[jax.ShapeDtypeStruct((n, d), BF16), jax.ShapeDtypeStruct((SUBLANES, n), jnp.int32),
                   jax.ShapeDtypeStruct((n, LANES), F32), jax.ShapeDtypeStruct((1, LANES), F32)],
        scratch_shapes=[pltpu.VMEM((1, LANES), F32)],
        compiler_params=_cparams("arbitrary"),
        name="router",
    )(x2, gain, wr, tril)


def _moe_kernel(ff_chunk, te_ref, tv_ref, x_ref, wg_ref, wu_ref, wd_ref, o_ref):
    t = pl.program_id(0)

    @pl.when(tv_ref[t] > 0)
    def _():
        y = _swiglu_tile(x_ref[...], wg_ref.at[0], wu_ref.at[0], wd_ref.at[0], ff_chunk)
        o_ref[...] = y.astype(o_ref.dtype)

    @pl.when(tv_ref[t] == 0)
    def _():
        o_ref[...] = jnp.zeros_like(o_ref)


def _moe_experts(xs, tile_expert, tile_valid, wg, wu, wd):
    p, d = xs.shape
    d_ff = wg.shape[2]
    tm = MOE_TILE
    grid_spec = pltpu.PrefetchScalarGridSpec(
        num_scalar_prefetch=2,
        grid=(p // tm,),
        in_specs=[pl.BlockSpec((tm, d), lambda t, te, tv: (t, 0)),
                  pl.BlockSpec((1, d, d_ff), lambda t, te, tv: (te[t], 0, 0)),
                  pl.BlockSpec((1, d, d_ff), lambda t, te, tv: (te[t], 0, 0)),
                  pl.BlockSpec((1, d_ff, d), lambda t, te, tv: (te[t], 0, 0))],
        out_specs=pl.BlockSpec((tm, d), lambda t, te, tv: (t, 0)),
    )
    return pl.pallas_call(
        functools.partial(_moe_kernel, _ffn_chunk(d_ff)),
        grid_spec=grid_spec,
        out_shape=jax.ShapeDtypeStruct((p, d), BF16),
        compiler_params=_cparams("arbitrary"),
        name="moe_experts",
    )(tile_expert, tile_valid, xs, wg, wu, wd)


def _combine_kernel(x_ref, wt_ref, y0_ref, y1_ref, o_ref):
    wt = wt_ref[...]
    o_ref[...] = x_ref[...] + (wt[:, 0:1] * y0_ref[...].astype(F32) + wt[:, 1:2] * y1_ref[...].astype(F32))


def _combine(x2, wt, y0, y1):
    n, d = x2.shape
    tm = min(ROW_TILE, n)
    row = lambda i: (i, 0)
    return pl.pallas_call(
        _combine_kernel,
        grid=(n // tm,),
        in_specs=[pl.BlockSpec((tm, d), row), pl.BlockSpec((tm, LANES), row),
                  pl.BlockSpec((tm, d), row), pl.BlockSpec((tm, d), row)],
        out_specs=pl.BlockSpec((tm, d), row),
        out_shape=jax.ShapeDtypeStruct((n, d), F32),
        compiler_params=_cparams("parallel"),
        name="moe_combine",
    )(x2, wt, y0, y1)


def _moe_ffn(x2, gain, w_router, expert_weights):
    n, d = x2.shape
    n_experts = w_router.shape[1]
    tm = MOE_TILE
    h, route, top_w, totals = _router(x2, gain, w_router)

    sizes = totals[0, :n_experts].astype(jnp.int32)
    padded = ((sizes + tm - 1) // tm) * tm
    starts = jnp.cumsum(padded) - padded
    pos = []
    for k in range(TOP_K):
        expert, rank = route[k], route[TOP_K + k]
        base = jnp.zeros_like(rank)
        for e in range(n_experts):
            base = base + jnp.where(expert == e, starts[e], 0)
        pos.append(rank + base)
    n_tiles = (n * TOP_K) // tm + n_experts
    p = n_tiles * tm
    token = jnp.tile(jnp.arange(n, dtype=jnp.int32), TOP_K)
    row_src = (jnp.arange(p, dtype=jnp.int32) % n).at[jnp.concatenate(pos)].set(
        token, unique_indices=True, mode="promise_in_bounds")
    tile_start = jnp.arange(n_tiles, dtype=jnp.int32) * tm
    ends = starts + padded
    tile_expert = jnp.minimum(jnp.sum((tile_start[:, None] >= ends[None, :]).astype(jnp.int32), axis=1),
                              n_experts - 1).astype(jnp.int32)
    tile_valid = (tile_start < ends[-1]).astype(jnp.int32)

    xs = h.at[row_src].get(mode="promise_in_bounds")
    ys = _moe_experts(xs, tile_expert, tile_valid, *expert_weights)
    y0 = ys.at[pos[0]].get(mode="promise_in_bounds")
    y1 = ys.at[pos[1]].get(mode="promise_in_bounds")
    return _combine(x2, top_w, y0, y1)


def kernel(x, attn_norm, w_in, b_gate, q_norm, k_norm, sinks, conv_w, conv_b, lru_w_r, lru_b_r, lru_w_i, lru_b_i, lru_lambda, w_proj_a, w_proj_b, w_proj_c, w_out, ffn_norm, w_ffn_gate, w_ffn_up, w_ffn_down, w_router, w_exp_gate, w_exp_up, w_exp_down):
    batch, seq, d = x.shape
    depth = w_in.shape[0]
    swa_q = w_proj_a.shape[1]
    sb_w = w_proj_b.shape[1]
    lru_w = w_proj_c.shape[1]
    swa_kv = SWA_KV_HEADS * HEAD_DIM
    widths = (swa_q, swa_kv, swa_kv, sb_w, sb_w, sb_w, lru_w, lru_w, N_BRANCHES * d)
    assert sum(widths) == w_in.shape[2]
    assert seq % SB_QBLK == 0 and seq % WINDOW == 0

    x2 = x.reshape(batch * seq, d).astype(F32)
    for l in range(depth):
        gain = attn_norm[l].astype(F32).reshape(1, d)
        qa, ka, va, qb, kb, vb, xc, gc, gates = _in_proj(x2, gain, _to_bf16(w_in, l), widths,
                                                         q_norm[l], k_norm[l])
        o_a = _swa(qa, ka, va, sinks[l], batch, seq)
        o_b = _stick_breaking(qb, kb, vb, batch, seq)
        o_c = _rglru(xc, gc, conv_w[l], conv_b[l], lru_w_r[l], lru_b_r[l], lru_w_i[l], lru_b_i[l],
                     lru_lambda[l], batch, seq)
        x2 = _merge(x2, o_a, o_b, o_c, gates, b_gate[l],
                    _to_bf16(w_proj_a, l), _to_bf16(w_proj_b, l), _to_bf16(w_proj_c, l),
                    _to_bf16(w_out, l))
        fgain = ffn_norm[l].astype(F32).reshape(1, d)
        j = l // 2
        if l % 2 == 0:
            side = [(w, j) for w in (w_exp_gate, w_exp_up, w_exp_down)] if l + 1 < depth else []
            x2, expert_bf16 = _dense_ffn(x2, fgain, _to_bf16(w_ffn_gate, j), _to_bf16(w_ffn_up, j),
                                         _to_bf16(w_ffn_down, j), side)
        else:
            x2 = _moe_ffn(x2, fgain, w_router[j], expert_bf16)
    return x2.reshape(batch, seq, d).astype(x.dtype)
```
